```python
import jax, jax.numpy as jnp
from jax import lax
import numpy as np

D_MODEL = 2048
BATCH = 4
SEQ = 2048
DEPTH = 1
DEC_BATCH = 128
DEC_SEQ = 4
PAST_LEN = 2048
PAGE_SIZE = 128

HEAD_DIM = 128
N_HEADS = D_MODEL // HEAD_DIM
N_KV = N_HEADS // 2
GROUP = N_HEADS // N_KV
Q_BLOCK = 128
N_EXPERTS = 32
TOP_K = 4
D_FF = D_MODEL
SWIGLU_LIMIT = 7.0
SWIGLU_ALPHA = 1.702
RMS_EPS = 1e-6
NEG_INF = -1e30
POOL_NUM, POOL_DEN = 5, 4

Q_W = N_HEADS * HEAD_DIM
KV_W = N_KV * HEAD_DIM
COL_SIZES = (Q_W, KV_W, KV_W, N_HEADS, Q_W, KV_W, KV_W, D_MODEL, D_MODEL)
SPLIT_IDX = tuple(int(v) for v in np.cumsum(COL_SIZES)[:-1])
N_IN = int(sum(COL_SIZES))

kernel_name = "fox_stickbreak_gated_moe_decoder_step"


def _rms(x, g):
    xf = x.astype(jnp.float32)
    y = xf * lax.rsqrt(jnp.mean(xf * xf, axis=-1, keepdims=True) + RMS_EPS)
    return y.astype(x.dtype) * g


def _modulation(c, w_ada, b_ada):
    m = jax.nn.silu(c) @ w_ada + b_ada
    return [t[:, None, :] for t in jnp.split(m, 6, axis=-1)]


def _project(h, w_in, b_forget):
    B, T, _ = h.shape
    z = h @ w_in
    qa, ka, va, fa, qb, kb, vb, ga, gb = jnp.split(z, SPLIT_IDX, axis=-1)
    logf = jax.nn.log_sigmoid((fa + b_forget).astype(jnp.float32))
    heads = lambda t, n: t.reshape(B, T, n, HEAD_DIM)
    return (heads(qa, N_HEADS), heads(ka, N_KV), heads(va, N_KV), logf,
            heads(qb, N_HEADS), heads(kb, N_KV), heads(vb, N_KV), ga, gb)


def _heads_first(f):
    B, T, _ = f.shape
    return f.transpose(0, 2, 1).reshape(B, N_KV, GROUP, T)


def _scores(q, ks):
    B, T = q.shape[:2]
    qg = q.reshape(B, T, N_KV, GROUP, HEAD_DIM) * (HEAD_DIM ** -0.5)
    return jnp.concatenate(
        [jnp.einsum('bqgrd,bsgd->bgrqs', qg, k, preferred_element_type=jnp.float32) for k in ks],
        axis=-1)


def _mix_values(w, vs):
    out, start = None, 0
    for v in vs:
        n = v.shape[1]
        part = jnp.einsum('bgrqs,bsgd->bqgrd', w[..., start:start + n].astype(v.dtype), v)
        out = part if out is None else out + part
        start += n
    B, T = out.shape[:2]
    return out.reshape(B, T, N_HEADS * HEAD_DIM)


def _fox_attend(q, fq, ks, vs, fks, q_pos, k_pos):
    s = _scores(q, ks)
    fk = jnp.concatenate(fks, axis=1)
    logits = s + _heads_first(fq)[..., :, None] - _heads_first(fk)[..., None, :]
    mask = k_pos[None, :] <= q_pos[:, None]
    p = jax.nn.softmax(jnp.where(mask, logits, NEG_INF), axis=-1)
    return _mix_values(p, vs)


def _sb_attend(q, ks, vs, q_pos, k_pos):
    z = _scores(q, ks)
    mask = k_pos[None, :] < q_pos[:, None]
    l = jnp.where(mask, jax.nn.log_sigmoid(-z), 0.0)
    suffix = lax.cumsum(l, axis=z.ndim - 1, reverse=True) - l
    w = jnp.where(mask, jnp.exp(jax.nn.log_sigmoid(z) + suffix), 0.0)
    return _mix_values(w, vs)


def _merge(oa, ob, ga, gb, w_out):
    return (jax.nn.sigmoid(ga) * oa + jax.nn.sigmoid(gb) * ob) @ w_out


def _mixer_prompt(h, w_in, b_forget, w_out):
    B, S, _ = h.shape
    qa, ka, va, logf, qb, kb, vb, ga, gb = _project(h, w_in, b_forget)
    fcum = lax.cumsum(logf, axis=1)
    k_pos = jnp.arange(S)

    def block(i):
        start = i * Q_BLOCK
        q_pos = start + jnp.arange(Q_BLOCK)
        sl = lambda t: lax.dynamic_slice_in_dim(t, start, Q_BLOCK, axis=1)
        oa = _fox_attend(sl(qa), sl(fcum), [ka], [va], [fcum], q_pos, k_pos)
        ob = _sb_attend(sl(qb), [kb], [vb], q_pos, k_pos)
        return oa, ob

    oa, ob = lax.map(block, jnp.arange(S // Q_BLOCK))
    unblock = lambda o: o.transpose(1, 0, 2, 3).reshape(B, S, -1)
    out = _merge(unblock(oa), unblock(ob), ga, gb, w_out)
    return out, (ka, va, logf, kb, vb)


def _gather_pages(pool, page_table):
    g = pool[page_table]
    return g.reshape((g.shape[0], g.shape[1] * g.shape[2]) + g.shape[3:])


def _mixer_sample(h, ck_fox, cv_fox, clf_fox, ck_sb, cv_sb, page_table, w_in, b_forget, w_out):
    T = h.shape[1]
    qa, ka, va, logf, qb, kb, vb, ga, gb = _project(h, w_in, b_forget)
    ka_p = _gather_pages(ck_fox, page_table)
    va_p = _gather_pages(cv_fox, page_table)
    lf_p = _gather_pages(clf_fox, page_table).astype(jnp.float32)
    kb_p = _gather_pages(ck_sb, page_table)
    vb_p = _gather_pages(cv_sb, page_table)
    P = ka_p.shape[1]
    f_past = lax.cumsum(lf_p, axis=1)
    f_new = f_past[:, -1:, :] + lax.cumsum(logf, axis=1)
    k_pos = jnp.arange(P + T)
    q_pos = P + jnp.arange(T)
    oa = _fox_attend(qa, f_new, [ka_p, ka], [va_p, va], [f_past, f_new], q_pos, k_pos)
    ob = _sb_attend(qb, [kb_p, kb], [vb_p, vb], q_pos, k_pos)
    out = _merge(oa, ob, ga, gb, w_out)
    return out, (ka, va, logf, kb, vb)


def _moe(h, w_router, b_router, w_up, b_up, w_down, b_down):
    B, T, D = h.shape
    x = h.reshape(B * T, D)
    logits = (x @ w_router + b_router).astype(jnp.float32)
    top_v, top_i = lax.top_k(logits, TOP_K)
    gates = jax.nn.softmax(top_v, axis=-1)
    combine = jnp.sum(jax.nn.one_hot(top_i, N_EXPERTS, dtype=jnp.float32) * gates[..., None], axis=1)

    def expert(acc, ew):
        w1, b1, w2, b2, cw = ew
        gu = x @ w1 + b1
        glu = jnp.minimum(gu[:, 0::2], SWIGLU_LIMIT)
        lin = jnp.clip(gu[:, 1::2], -SWIGLU_LIMIT, SWIGLU_LIMIT)
        act = glu * jax.nn.sigmoid(SWIGLU_ALPHA * glu) * (lin + 1.0)
        y = act @ w2 + b2
        return acc + cw[:, None].astype(y.dtype) * y, None

    acc, _ = lax.scan(expert, jnp.zeros_like(x), (w_up, b_up, w_down, b_down, combine.T))
    return acc.reshape(B, T, D)


def _layer(x, mod, mixer_fn, n_pre_mix, n_post_mix, n_pre_ffn, n_post_ffn, moe_fn):
    sh1, sc1, g1, sh2, sc2, g2 = mod
    h = _rms(x, n_pre_mix) * (1.0 + sc1) + sh1
    m, state = mixer_fn(h)
    x = x + g1 * _rms(m, n_post_mix)
    h = _rms(x, n_pre_ffn) * (1.0 + sc2) + sh2
    x = x + g2 * _rms(moe_fn(h), n_post_ffn)
    return x, state


def setup_inputs(seed: int = 0) -> dict:
    key = jax.random.key(seed)
    ks = jax.random.split(key, 32)
    f32 = jnp.float32
    D = D_MODEL
    nrm = lambda k, shape, scale=1.0: scale * jax.random.normal(k, shape, f32)
    n_pages = PAST_LEN // PAGE_SIZE
    n_pool = (DEC_BATCH * n_pages * POOL_NUM) // POOL_DEN
    page_table = jax.random.permutation(ks[0], n_pool)[:DEC_BATCH * n_pages]
    page_table = page_table.reshape(DEC_BATCH, n_pages).astype(jnp.int32)
    kv_shape = (DEPTH, n_pool, PAGE_SIZE, N_KV, HEAD_DIM)
    return {
        "x_prompt": nrm(ks[1], (BATCH, SEQ, D)),
        "x_sample": nrm(ks[2], (DEC_BATCH, DEC_SEQ, D)),
        "c_prompt": nrm(ks[3], (BATCH, D)),
        "c_sample": nrm(ks[4], (DEC_BATCH, D)),
        "cache_k_fox": nrm(ks[5], kv_shape),
        "cache_v_fox": nrm(ks[6], kv_shape),
        "cache_logf_fox": jax.nn.log_sigmoid(2.0 + nrm(ks[7], (DEPTH, n_pool, PAGE_SIZE, N_HEADS), 0.5)),
        "cache_k_sb": nrm(ks[8], kv_shape),
        "cache_v_sb": nrm(ks[9], kv_shape),
        "page_table": page_table,
        "w_ada": nrm(ks[10], (DEPTH, D, 6 * D), 0.5 * D ** -0.5),
        "b_ada": nrm(ks[11], (DEPTH, 6 * D), 0.02),
        "norm_pre_mix": 1.0 + nrm(ks[12], (DEPTH, D), 0.05),
        "norm_post_mix": 1.0 + nrm(ks[13], (DEPTH, D), 0.05),
        "norm_pre_ffn": 1.0 + nrm(ks[14], (DEPTH, D), 0.05),
        "norm_post_ffn": 1.0 + nrm(ks[15], (DEPTH, D), 0.05),
        "w_in": nrm(ks[16], (DEPTH, D, N_IN), D ** -0.5),
        "b_forget": 2.0 + nrm(ks[17], (DEPTH, N_HEADS), 0.5),
        "w_out": nrm(ks[18], (DEPTH, N_HEADS * HEAD_DIM, D), (N_HEADS * HEAD_DIM) ** -0.5),
        "w_router": nrm(ks[19], (DEPTH, D, N_EXPERTS), D ** -0.5),
        "b_router": nrm(ks[20], (DEPTH, N_EXPERTS), 0.01),
        "w_up": nrm(ks[21], (DEPTH, N_EXPERTS, D, 2 * D_FF), D ** -0.5),
        "b_up": nrm(ks[22], (DEPTH, N_EXPERTS, 2 * D_FF), 0.02),
        "w_down": nrm(ks[23], (DEPTH, N_EXPERTS, D_FF, D), D_FF ** -0.5),
        "b_down": nrm(ks[24], (DEPTH, N_EXPERTS, D), 0.02),
    }


def reference(x_prompt, x_sample, c_prompt, c_sample, cache_k_fox, cache_v_fox, cache_logf_fox,
              cache_k_sb, cache_v_sb, page_table, w_ada, b_ada, norm_pre_mix, norm_post_mix,
              norm_pre_ffn, norm_post_ffn, w_in, b_forget, w_out, w_router, b_router,
              w_up, b_up, w_down, b_down):
    xp, xs = x_prompt, x_sample
    p_states, s_states = [], []
    for l in range(DEPTH):
        norms = (norm_pre_mix[l], norm_post_mix[l], norm_pre_ffn[l], norm_post_ffn[l])
        moe_fn = lambda h: _moe(h, w_router[l], b_router[l], w_up[l], b_up[l], w_down[l], b_down[l])
        xp, sp = _layer(
            xp, _modulation(c_prompt, w_ada[l], b_ada[l]),
            lambda h: _mixer_prompt(h, w_in[l], b_forget[l], w_out[l]),
            *norms, moe_fn)
        xs, ss = _layer(
            xs, _modulation(c_sample, w_ada[l], b_ada[l]),
            lambda h: _mixer_sample(h, cache_k_fox[l], cache_v_fox[l], cache_logf_fox[l],
                                    cache_k_sb[l], cache_v_sb[l], page_table,
                                    w_in[l], b_forget[l], w_out[l]),
            *norms, moe_fn)
        p_states.append(ss if False else sp)
        s_states.append(ss)
    k_fox_p, v_fox_p, logf_fox_p, k_sb_p, v_sb_p = (jnp.stack(t) for t in zip(*p_states))
    k_fox_s, v_fox_s, logf_fox_s, k_sb_s, v_sb_s = (jnp.stack(t) for t in zip(*s_states))
    return (xp, xs, k_fox_p, v_fox_p, logf_fox_p, k_sb_p, v_sb_p,
            k_fox_s, v_fox_s, logf_fox_s, k_sb_s, v_sb_s)
```

```python
import functools

import jax
import jax.numpy as jnp
from jax import lax
from jax.experimental import pallas as pl
from jax.experimental.pallas import tpu as pltpu

F32 = jnp.float32
BF16 = jnp.bfloat16
I32 = jnp.int32

HEAD_DIM = 128
GROUP = 2
TOP_K = 4
SWIGLU_LIMIT = 7.0
SWIGLU_ALPHA = 1.702
RMS_EPS = 1e-6
NEG_INF = -1e30
EXP_UNDERFLOW = -104.0
LANES = 128
VMEM_LIMIT_BYTES = 48 * 1024 * 1024

ROW_TILE = 512
PROJ_COLS = 1024
ATTN_TILE = 256
POST_TILE = 256
MOE_TILE = 256
MOE_UP_COLS = 1024
MOE_DOWN_COLS = 1024
COMBINE_TILE = 128


def _cparams(*sem):
    return pltpu.CompilerParams(dimension_semantics=sem, vmem_limit_bytes=VMEM_LIMIT_BYTES)


def _sigmoid(x):
    return 1.0 / (1.0 + jnp.exp(-x))


def _log_sigmoid(x):
    return jnp.minimum(x, 0.0) - jnp.log1p(jnp.exp(-jnp.abs(x)))


def _rms(x):
    return x * lax.rsqrt(jnp.mean(x * x, axis=-1, keepdims=True) + RMS_EPS)


def _split_bf16(x, parts):
    out = []
    for _ in range(parts - 1):
        p = x.astype(BF16)
        out.append(p)
        x = x - p.astype(F32)
    out.append(x.astype(BF16))
    return out


def _dot_nt(a, b):
    return lax.dot_general(a, b, (((1,), (1,)), ((), ())), preferred_element_type=F32)


def _dot(a, b):
    return jnp.dot(a, b, preferred_element_type=F32)


def _strict_lower_ones(n):
    r = lax.broadcasted_iota(I32, (n, n), 0)
    c = lax.broadcasted_iota(I32, (n, n), 1)
    return jnp.where(r > c, 1.0, 0.0).astype(BF16)


def _ada_kernel(c_ref, w_ref, b_ref, o_ref):
    c = c_ref[...]
    s = (c * _sigmoid(c)).astype(BF16)
    o_ref[...] = _dot(s, w_ref[...].astype(BF16)) + b_ref[...]


def _modulation(c_all, w_ada, b_ada):
    m, d = c_all.shape
    n6 = w_ada.shape[1]
    tn = PROJ_COLS
    return pl.pallas_call(
        _ada_kernel,
        out_shape=jax.ShapeDtypeStruct((m, n6), F32),
        grid=(n6 // tn,),
        in_specs=[pl.BlockSpec((m, d), lambda j: (0, 0)),
                  pl.BlockSpec((d, tn), lambda j: (0, j)),
                  pl.BlockSpec((1, tn), lambda j: (0, j))],
        out_specs=pl.BlockSpec((m, tn), lambda j: (0, j)),
        compiler_params=_cparams("arbitrary"),
        name="ada",
    )(c_all, w_ada, b_ada.reshape(1, n6))


class _Mod:
    def __init__(self, array, spec_fn):
        self.array = array
        self.spec = spec_fn


def _prompt_mod(mod_p, seq, d):
    arr = mod_p.reshape(mod_p.shape[0], 1, 6 * d)

    def spec(k, tm):
        per_seq = seq // tm
        return pl.BlockSpec((None, 1, d), lambda i, *_: (i // per_seq, 0, k))
    return _Mod(arr, spec)


def _sample_mod(mod_s, t, d):
    arr = jnp.repeat(mod_s, t, axis=0)

    def spec(k, tm):
        return pl.BlockSpec((tm, d), lambda i, *_: (i, k))
    return _Mod(arr, spec)


def _prenorm_kernel(x_ref, g_ref, sh_ref, sc_ref, wf_ref, bf_ref, h_ref, lf_ref):
    h = _rms(x_ref[...]) * g_ref[...] * (1.0 + sc_ref[...]) + sh_ref[...]
    hb = h.astype(BF16)
    h_ref[...] = hb
    fa = _dot(hb, wf_ref[...]) + bf_ref[...]
    lf_ref[...] = _log_sigmoid(fa)[:, :lf_ref.shape[1]]


def _prenorm(x, mod, g, wf_pad, bf_pad, n_heads):
    n, d = x.shape
    tm = min(ROW_TILE, n)
    return pl.pallas_call(
        _prenorm_kernel,
        out_shape=(jax.ShapeDtypeStruct((n, d), BF16), jax.ShapeDtypeStruct((n, n_heads), F32)),
        grid=(n // tm,),
        in_specs=[pl.BlockSpec((tm, d), lambda i: (i, 0)),
                  pl.BlockSpec((1, d), lambda i: (0, 0)),
                  mod.spec(0, tm), mod.spec(1, tm),
                  pl.BlockSpec((d, LANES), lambda i: (0, 0)),
                  pl.BlockSpec((1, LANES), lambda i: (0, 0))],
        out_specs=(pl.BlockSpec((tm, d), lambda i: (i, 0)),
                   pl.BlockSpec((tm, n_heads), lambda i: (i, 0))),
        compiler_params=_cparams("arbitrary"),
        name="prenorm",
    )(x, g, mod.array, mod.array, wf_pad, bf_pad)


def _proj_kernel(h_ref, w_ref, *o_refs, scale):
    acc = _dot(h_ref[...], w_ref[...])
    if scale is not None:
        acc = acc * scale
    for o_ref in o_refs:
        o_ref[...] = acc.astype(o_ref.dtype)


def _project(h, w_bf, col0, width, out_dtypes, scale=None):
    n, d = h.shape
    tm = min(ROW_TILE, n)
    tn = min(PROJ_COLS, width)
    cb0 = col0 // tn
    outs = pl.pallas_call(
        functools.partial(_proj_kernel, scale=scale),
        out_shape=tuple(jax.ShapeDtypeStruct((n, width), dt) for dt in out_dtypes),
        grid=(width // tn, n // tm),
        in_specs=[pl.BlockSpec((tm, d), lambda j, i: (i, 0)),
                  pl.BlockSpec((d, tn), lambda j, i: (0, cb0 + j))],
        out_specs=tuple(pl.BlockSpec((tm, tn), lambda j, i: (i, j)) for _ in out_dtypes),
        compiler_params=_cparams("arbitrary", "arbitrary"),
        name="proj",
    )(h, w_bf)
    return outs


def _cumsum_kernel(lf_ref, o_ref):
    nh, s = lf_ref.shape
    blk = LANES
    r = lax.broadcasted_iota(I32, (blk, blk), 0)
    c = lax.broadcasted_iota(I32, (blk, blk), 1)
    upper = jnp.where(r <= c, 1.0, 0.0).astype(BF16)

    def body(i, carry):
        off = pl.multiple_of(i * blk, blk)
        x = lf_ref[:, pl.ds(off, blk)]
        cs = carry
        for p in _split_bf16(x, 3):
            cs = cs + _dot(p, upper)
        o_ref[:, pl.ds(off, blk)] = cs
        return cs[:, blk - 1:blk]

    lax.fori_loop(0, s // blk, body, jnp.zeros((nh, 1), F32))


def _cumsum_rows(lf_t):
    b, nh, s = lf_t.shape
    return pl.pallas_call(
        _cumsum_kernel,
        out_shape=jax.ShapeDtypeStruct((b, nh, s), F32),
        grid=(b,),
        in_specs=[pl.BlockSpec((None, nh, s), lambda i: (i, 0, 0))],
        out_specs=pl.BlockSpec((None, nh, s), lambda i: (i, 0, 0)),
        compiler_params=_cparams("arbitrary"),
        name="cumsum",
    )(lf_t)


def _prompt_attn_kernel(qa_ref, qb_ref, ka_ref, va_ref, kb_ref, vb_ref, fr_ref, ga_ref, gb_ref, o_ref):
    tq = qa_ref.shape[0]
    hd = HEAD_DIM
    g = pl.program_id(1)
    qi = pl.program_id(2)
    q0 = pl.multiple_of(qi * tq, tq)
    row = lax.broadcasted_iota(I32, (tq, tq), 0)
    col = lax.broadcasted_iota(I32, (tq, tq), 1)
    incl = col <= row
    strict = col < row
    suffix_ones = _strict_lower_ones(tq)

    for r in range(GROUP):
        h = g * GROUP + r
        qa = qa_ref[:, r * hd:(r + 1) * hd]
        qb = qb_ref[:, r * hd:(r + 1) * hd]
        f_ref0 = fr_ref[pl.ds(h, 1), pl.ds(q0, tq)][:, 0:1]

        def fox_tile(kj, carry, diag):
            m, l, acc = carry
            k0 = pl.multiple_of(kj * tq, tq)
            s = _dot_nt(qa, ka_ref[pl.ds(k0, tq), :])
            s = s + (f_ref0 - fr_ref[pl.ds(h, 1), pl.ds(k0, tq)])
            if diag:
                s = jnp.where(incl, s, NEG_INF)
            m_new = jnp.maximum(m, jnp.max(s, axis=1, keepdims=True))
            alpha = jnp.exp(m - m_new)
            p = jnp.exp(s - m_new)
            l = alpha * l + jnp.sum(p, axis=1, keepdims=True)
            acc = alpha * acc + _dot(p.astype(BF16), va_ref[pl.ds(k0, tq), :])
            return m_new, l, acc

        init = (jnp.full((tq, 1), NEG_INF, F32), jnp.zeros((tq, 1), F32), jnp.zeros((tq, hd), F32))
        carry = lax.fori_loop(0, qi, lambda kj, cr: fox_tile(kj, cr, False), init)
        _, l, acc = fox_tile(qi, carry, True)
        oa = acc / l

        def sb_tile(kj, c, acc, diag):
            k0 = pl.multiple_of(kj * tq, tq)
            z = _dot_nt(qb, kb_ref[pl.ds(k0, tq), :])
            t = jnp.log1p(jnp.exp(-jnp.abs(z)))
            lneg = -jnp.maximum(z, 0.0) - t
            lpos = jnp.minimum(z, 0.0) - t
            if diag:
                lneg = jnp.where(strict, lneg, 0.0)
            suf = c
            for p in _split_bf16(lneg, 2):
                suf = suf + _dot(p, suffix_ones)
            w = jnp.exp(lpos + suf)
            if diag:
                w = jnp.where(strict, w, 0.0)
            acc = acc + _dot(w.astype(BF16), vb_ref[pl.ds(k0, tq), :])
            c = c + jnp.sum(lneg, axis=1, keepdims=True)
            return c, acc

        c, acc = sb_tile(qi, jnp.zeros((tq, 1), F32), jnp.zeros((tq, hd), F32), True)

        def cond(st):
            return jnp.logical_and(st[0] >= 0, jnp.max(st[1]) > EXP_UNDERFLOW)

        def body(st):
            c2, acc2 = sb_tile(st[0], st[1], st[2], False)
            return st[0] - 1, c2, acc2

        _, _, ob = lax.while_loop(cond, body, (qi - 1, c, acc))

        ga = ga_ref[:, r * hd:(r + 1) * hd]
        gb = gb_ref[:, r * hd:(r + 1) * hd]
        o_ref[:, r * hd:(r + 1) * hd] = (_sigmoid(ga) * oa + _sigmoid(gb) * ob).astype(o_ref.dtype)


def _prompt_attention(qa, qb, ka, va, kb, vb, f_rows, gates, batch, seq, n_kv):
    n, dq = qa.shape
    tq = min(ATTN_TILE, seq)
    nq = seq // tq
    nh = f_rows.shape[1]
    gw = GROUP * HEAD_DIM
    qspec = pl.BlockSpec((tq, gw), lambda b, g, i: (b * nq + i, g))
    kvspec = pl.BlockSpec((seq, HEAD_DIM), lambda b, g, i: (b, g))
    return pl.pallas_call(
        _prompt_attn_kernel,
        out_shape=jax.ShapeDtypeStruct((n, dq), BF16),
        grid=(batch, n_kv, nq),
        in_specs=[qspec, qspec, kvspec, kvspec, kvspec, kvspec,
                  pl.BlockSpec((None, nh, seq), lambda b, g, i: (b, 0, 0)),
                  pl.BlockSpec((tq, gw), lambda b, g, i: (b * nq + i, g)),
                  pl.BlockSpec((tq, gw), lambda b, g, i: (b * nq + i, n_kv + g))],
        out_specs=qspec,
        compiler_params=_cparams("arbitrary", "arbitrary", "arbitrary"),
        name="prompt_attn",
    )(qa, qb, ka, va, kb, vb, f_rows, gates, gates)


def _block_diag_queries(q, n_heads, n_kv):
    rows = q.shape[0]
    kv_of_row = (lax.broadcasted_iota(I32, (rows, 1), 0) % n_heads) // GROUP
    blocks = [jnp.where(kv_of_row == g, q, jnp.zeros_like(q)) for g in range(n_kv)]
    return jnp.concatenate(blocks, axis=1), kv_of_row


def _own_head_block(pv, kv_of_row, n_kv):
    out = jnp.zeros((pv.shape[0], HEAD_DIM), F32)
    for g in range(n_kv):
        out = out + jnp.where(kv_of_row == g, pv[:, g * HEAD_DIM:(g + 1) * HEAD_DIM], 0.0)
    return out


def _page_copies(pt_ref, b, p, slot, kc_ref, vc_ref, kbuf, vbuf, sem):
    page = pt_ref[b, p]
    return (pltpu.make_async_copy(kc_ref.at[page], kbuf.at[slot], sem.at[0, slot]),
            pltpu.make_async_copy(vc_ref.at[page], vbuf.at[slot], sem.at[1, slot]))


def _load_page(buf, slot, n_kv):
    tokens = buf.shape[1] // n_kv
    heads = [buf[slot, pl.ds(g, tokens, stride=n_kv), :].astype(BF16) for g in range(n_kv)]
    return jnp.concatenate(heads, axis=1)


def _sample_fox_kernel(pt_ref, q_ref, kn_ref, vn_ref, bn_ref, *rest, n_pages, n_heads, n_kv, n_new):
    lf_refs = rest[:n_pages]
    kc_ref, vc_ref, o_ref, kbuf, vbuf, bias_s, sem = rest[n_pages:]
    b = pl.program_id(0)
    page = kbuf.shape[1] // n_kv
    rows = q_ref.shape[0]
    last = n_pages - 1

    def copies(p, slot):
        return _page_copies(pt_ref, b, p, slot, kc_ref, vc_ref, kbuf, vbuf, sem)

    for cp in copies(last, last % 2):
        cp.start()

    x = jnp.concatenate([lf_refs[p][...] for p in range(n_pages)], axis=0)
    within = jnp.zeros(x.shape, F32)
    ones = _strict_lower_ones(page)
    for piece in _split_bf16(x, 3):
        within = within + _dot(piece, ones)
    totals = jnp.sum(x, axis=1, keepdims=True)
    later = jnp.zeros((n_heads, 1), F32)
    for p in range(last, -1, -1):
        bias_s[p] = within[p * n_heads:(p + 1) * n_heads] + later
        later = later + totals[p * n_heads:(p + 1) * n_heads]

    qbd, kv_of_row = _block_diag_queries(q_ref[...], n_heads, n_kv)
    t_of_row = lax.broadcasted_iota(I32, (rows, page), 0) // n_heads
    lane = lax.broadcasted_iota(I32, (rows, page), 1)

    def attend(kp, vp, bias, mask, carry):
        m, l, acc = carry
        s = _dot_nt(qbd, kp) + bias
        if mask is not None:
            s = jnp.where(mask, s, NEG_INF)
        m_new = jnp.maximum(m, jnp.max(s, axis=1, keepdims=True))
        alpha = jnp.exp(m - m_new)
        p = jnp.exp(s - m_new)
        l = alpha * l + jnp.sum(p, axis=1, keepdims=True)
        pv = _dot(p.astype(BF16), vp)
        acc = alpha * acc + _own_head_block(pv, kv_of_row, n_kv)
        return m_new, l, acc

    def pad_page(ref):
        new = ref[...]
        return jnp.concatenate([new, jnp.zeros((page - new.shape[0], new.shape[1]), F32)], axis=0).astype(BF16)

    init = (jnp.full((rows, 1), NEG_INF, F32), jnp.zeros((rows, 1), F32), jnp.zeros((rows, HEAD_DIM), F32))
    bias_new = jnp.concatenate([bn_ref[...]] * n_new, axis=0)
    carry = attend(pad_page(kn_ref), pad_page(vn_ref), bias_new, lane <= t_of_row, init)

    def body(i, carry):
        p = last - i
        slot = p % 2
        for cp in copies(p, slot):
            cp.wait()

        @pl.when(p > 0)
        def _():
            for cp in copies(p - 1, 1 - slot):
                cp.start()

        bias = jnp.concatenate([bias_s[p]] * n_new, axis=0)
        return attend(_load_page(kbuf, slot, n_kv), _load_page(vbuf, slot, n_kv), bias, None, carry)

    _, l, acc = lax.fori_loop(0, n_pages, body, carry)
    o_ref[...] = acc / l


def _sample_sb_kernel(pt_ref, q_ref, kn_ref, vn_ref, oa_ref, gate_ref, kc_ref, vc_ref, o_ref,
                      kbuf, vbuf, sem, *, n_pages, n_heads, n_kv, n_new):
    b = pl.program_id(0)
    page = kbuf.shape[1] // n_kv
    rows = q_ref.shape[0]
    last = n_pages - 1

    def copies(p, slot):
        return _page_copies(pt_ref, b, p, slot, kc_ref, vc_ref, kbuf, vbuf, sem)

    for cp in copies(last, last % 2):
        cp.start()

    qbd, kv_of_row = _block_diag_queries(q_ref[...], n_heads, n_kv)
    t_of_row = lax.broadcasted_iota(I32, (rows, page), 0) // n_heads
    lane = lax.broadcasted_iota(I32, (rows, page), 1)
    ones = _strict_lower_ones(page)

    def attend(kp, vp, mask, c, acc):
        z = _dot_nt(qbd, kp)
        t = jnp.log1p(jnp.exp(-jnp.abs(z)))
        lneg = -jnp.maximum(z, 0.0) - t
        lpos = jnp.minimum(z, 0.0) - t
        if mask is not None:
            lneg = jnp.where(mask, lneg, 0.0)
        suf = c
        for piece in _split_bf16(lneg, 2):
            suf = suf + _dot(piece, ones)
        w = jnp.exp(lpos + suf)
        if mask is not None:
            w = jnp.where(mask, w, 0.0)
        acc = acc + _own_head_block(_dot(w.astype(BF16), vp), kv_of_row, n_kv)
        c = c + jnp.sum(lneg, axis=1, keepdims=True)
        return c, acc

    def pad_page(ref):
        new = ref[...]
        return jnp.concatenate([new, jnp.zeros((page - new.shape[0], new.shape[1]), F32)], axis=0).astype(BF16)

    c, acc = attend(pad_page(kn_ref), pad_page(vn_ref), lane < t_of_row,
                    jnp.zeros((rows, 1), F32), jnp.zeros((rows, HEAD_DIM), F32))

    def cond(st):
        return jnp.logical_and(st[0] >= 0, jnp.max(st[1]) > EXP_UNDERFLOW)

    def body(st):
        p = st[0]
        slot = p % 2
        for cp in copies(p, slot):
            cp.wait()

        @pl.when(p > 0)
        def _():
            for cp in copies(p - 1, 1 - slot):
                cp.start()

        c2, acc2 = attend(_load_page(kbuf, slot, n_kv), _load_page(vbuf, slot, n_kv), None, st[1], st[2])
        return p - 1, c2, acc2

    p_end, _, ob = lax.while_loop(cond, body, (jnp.int32(last), c, acc))

    @pl.when(p_end >= 0)
    def _():
        for cp in copies(jnp.maximum(p_end, 0), jnp.maximum(p_end, 0) % 2):
            cp.wait()

    ga = jnp.concatenate([gate_ref[t * 2 * n_heads:t * 2 * n_heads + n_heads, :] for t in range(n_new)], axis=0)
    gb = jnp.concatenate([gate_ref[t * 2 * n_heads + n_heads:(t + 1) * 2 * n_heads, :] for t in range(n_new)], axis=0)
    o_ref[...] = (_sigmoid(ga) * oa_ref[...] + _sigmoid(gb) * ob).astype(o_ref.dtype)


def _sample_attention(qa, qb, ka_new, va_new, kb_new, vb_new, bias_new, gates, lf_pool_t,
                      ck_fox, cv_fox, ck_sb, cv_sb, page_table, n_heads, n_kv, n_new):
    bs, n_pages = page_table.shape
    pool, page_rows, _ = ck_fox.shape
    page = page_rows // n_kv
    w = n_kv * HEAD_DIM
    rows = n_new * n_heads
    seq_rows = pl.BlockSpec((None, rows, HEAD_DIM), lambda b, pt: (b, 0, 0))
    new_kv = pl.BlockSpec((None, ka_new.shape[1], w), lambda b, pt: (b, 0, 0))
    any_spec = pl.BlockSpec(memory_space=pl.ANY)
    scratch = [pltpu.VMEM((2, page_rows, HEAD_DIM), F32), pltpu.VMEM((2, page_rows, HEAD_DIM), F32)]

    lf_specs = [pl.BlockSpec((None, n_heads, page), functools.partial(lambda b, pt, p: (pt[b, p], 0, 0), p=p))
                for p in range(n_pages)]
    oa = pl.pallas_call(
        functools.partial(_sample_fox_kernel, n_pages=n_pages, n_heads=n_heads, n_kv=n_kv, n_new=n_new),
        out_shape=jax.ShapeDtypeStruct((bs, rows, HEAD_DIM), F32),
        grid_spec=pltpu.PrefetchScalarGridSpec(
            num_scalar_prefetch=1, grid=(bs,),
            in_specs=[seq_rows, new_kv, new_kv,
                      pl.BlockSpec((None, n_heads, page), lambda b, pt: (b, 0, 0))] + lf_specs + [any_spec, any_spec],
            out_specs=seq_rows,
            scratch_shapes=scratch + [pltpu.VMEM((n_pages, n_heads, page), F32), pltpu.SemaphoreType.DMA((2, 2))]),
        compiler_params=_cparams("arbitrary"),
        name="sample_fox",
    )(page_table, qa, ka_new, va_new, bias_new, *([lf_pool_t] * n_pages), ck_fox, cv_fox)

    return pl.pallas_call(
        functools.partial(_sample_sb_kernel, n_pages=n_pages, n_heads=n_heads, n_kv=n_kv, n_new=n_new),
        out_shape=jax.ShapeDtypeStruct((bs, rows, HEAD_DIM), BF16),
        grid_spec=pltpu.PrefetchScalarGridSpec(
            num_scalar_prefetch=1, grid=(bs,),
            in_specs=[seq_rows, new_kv, new_kv, seq_rows,
                      pl.BlockSpec((None, 2 * rows, HEAD_DIM), lambda b, pt: (b, 0, 0)), any_spec, any_spec],
            out_specs=seq_rows,
            scratch_shapes=scratch + [pltpu.SemaphoreType.DMA((2, 2))]),
        compiler_params=_cparams("arbitrary"),
        name="sample_sb",
    )(page_table, qb, kb_new, vb_new, oa, gates, ck_sb, cv_sb)


def _post_kernel(o_ref, wo_ref, x_ref, g1_ref, sh2_ref, sc2_ref, npm_ref, npf_ref, wr_ref, br_ref,
                 x1_ref, h2_ref, ti_ref, tg_ref):
    m = _dot(o_ref[...], wo_ref[...])
    x1 = x_ref[...] + g1_ref[...] * (_rms(m) * npm_ref[...])
    x1_ref[...] = x1
    h2 = _rms(x1) * npf_ref[...] * (1.0 + sc2_ref[...]) + sh2_ref[...]
    h2_ref[...] = h2
    logits = _dot(h2.astype(BF16), wr_ref[...]) + br_ref[...]
    lane = lax.broadcasted_iota(I32, logits.shape, 1)
    lane_f = lane.astype(F32)
    work = logits
    vals, ids = [], []
    for _ in range(TOP_K):
        mk = jnp.max(work, axis=1, keepdims=True)
        ik = jnp.min(jnp.where(work == mk, lane_f, float(LANES)), axis=1, keepdims=True)
        vals.append(mk)
        ids.append(ik)
        work = jnp.where(lane_f == ik, -jnp.inf, work)
    es = [jnp.exp(v - vals[0]) for v in vals]
    den = es[0]
    for e in es[1:]:
        den = den + e
    ti = jnp.zeros(logits.shape, F32)
    tg = jnp.zeros(logits.shape, F32)
    for k in range(TOP_K):
        ti = jnp.where(lane == k, ids[k], ti)
        tg = jnp.where(lane == k, es[k] / den, tg)
    ti_ref[...] = ti.astype(I32)
    tg_ref[...] = tg


def _post_attention(o, wo_bf, x, mod, npm, npf, wr_pad, br_pad):
    n, d = x.shape
    tm = min(POST_TILE, n)
    row = pl.BlockSpec((tm, d), lambda i: (i, 0))
    vec = pl.BlockSpec((1, d), lambda i: (0, 0))
    small = pl.BlockSpec((tm, LANES), lambda i: (i, 0))
    return pl.pallas_call(
        _post_kernel,
        out_shape=(jax.ShapeDtypeStruct((n, d), F32), jax.ShapeDtypeStruct((n, d), F32),
                   jax.ShapeDtypeStruct((n, LANES), I32), jax.ShapeDtypeStruct((n, LANES), F32)),
        grid=(n // tm,),
        in_specs=[row, pl.BlockSpec((d, d), lambda i: (0, 0)), row,
                  mod.spec(2, tm), mod.spec(3, tm), mod.spec(4, tm), vec, vec,
                  pl.BlockSpec((d, LANES), lambda i: (0, 0)), pl.BlockSpec((1, LANES), lambda i: (0, 0))],
        out_specs=(row, row, small, small),
        compiler_params=_cparams("arbitrary"),
        name="post_attn",
    )(o, wo_bf, x, mod.array, mod.array, mod.array, npm, npf, wr_pad, br_pad)


def _route_plan(top_i, top_g, n_experts, tm, n_tiles):
    n = top_i.shape[0]
    e_flat = top_i.reshape(-1)
    onehot = (e_flat[:, None] == jnp.arange(n_experts, dtype=I32)[None, :]).astype(I32)
    csum = jnp.cumsum(onehot, axis=0)
    rank = jnp.sum((csum - onehot) * onehot, axis=1)
    counts = csum[-1]
    tiles_e = (counts + tm - 1) // tm
    tile_end = jnp.cumsum(tiles_e)
    tile_start = tile_end - tiles_e
    n_used = tile_end[-1:].astype(I32)
    dest = (tile_start[e_flat] * tm + rank).astype(I32)
    tile_expert = jnp.minimum(jnp.searchsorted(tile_end, jnp.arange(n_tiles, dtype=I32), side="right"),
                              n_experts - 1).astype(I32)
    row_token = jnp.zeros((n_tiles * tm,), I32).at[dest].set(jnp.arange(n * TOP_K, dtype=I32) // TOP_K)
    row_gate = jnp.zeros((n_tiles * tm,), F32).at[dest].set(top_g.reshape(-1))
    return dest.reshape(n, TOP_K), tile_expert, n_used, row_token, row_gate


def _row_copy(src_hbm, src_row, dst, dst_row, sem):
    return pltpu.make_async_copy(src_hbm.at[pl.ds(src_row, 1), :], dst.at[pl.ds(dst_row, 1), :], sem)


def _moe_gather_kernel(nu_ref, tok_ref, tok_next_ref, h_hbm, xs_ref, buf, sem):
    i = pl.program_id(0)
    n_used = nu_ref[0]
    tm = xs_ref.shape[0]

    def issue(idx_ref, slot):
        def body(r, carry):
            _row_copy(h_hbm, idx_ref[0, r], buf.at[slot], r, sem.at[slot]).start()
            return carry
        lax.fori_loop(0, tm, body, 0, unroll=8)

    @pl.when(i == 0)
    def _():
        issue(tok_ref, 0)

    @pl.when(i + 1 < n_used)
    def _():
        issue(tok_next_ref, (i + 1) % 2)

    @pl.when(i < n_used)
    def _():
        slot = i % 2
        pltpu.make_async_copy(h_hbm.at[pl.ds(0, tm), :], buf.at[slot], sem.at[slot]).wait()
        xs_ref[...] = buf[slot].astype(xs_ref.dtype)

    @pl.when(i >= n_used)
    def _():
        xs_ref[...] = jnp.zeros(xs_ref.shape, xs_ref.dtype)


def _moe_gather(h2, row_token, n_used, tm, n_tiles):
    n, d = h2.shape
    tok3 = row_token.reshape(n_tiles, 1, tm)
    last = n_tiles - 1
    return pl.pallas_call(
        _moe_gather_kernel,
        out_shape=jax.ShapeDtypeStruct((n_tiles * tm, d), BF16),
        grid_spec=pltpu.PrefetchScalarGridSpec(
            num_scalar_prefetch=1, grid=(n_tiles,),
            in_specs=[pl.BlockSpec((None, 1, tm), lambda i, nu: (i, 0, 0), memory_space=pltpu.SMEM),
                      pl.BlockSpec((None, 1, tm), lambda i, nu: (jnp.minimum(i + 1, last), 0, 0),
                                   memory_space=pltpu.SMEM),
                      pl.BlockSpec(memory_space=pl.ANY)],
            out_specs=pl.BlockSpec((tm, d), lambda i, nu: (i, 0)),
            scratch_shapes=[pltpu.VMEM((2, tm, d), F32), pltpu.SemaphoreType.DMA((2,))]),
        compiler_params=_cparams("arbitrary"),
        name="moe_gather",
    )(n_used, tok3, tok3, h2)


def _new_expert(te_ref, t):
    prev = te_ref[jnp.maximum(t - 1, 0)]
    return jnp.logical_or(t == 0, te_ref[t] != prev)


def _moe_up_kernel(te_ref, nu_ref, xs_ref, w_ref, b_ref, act_ref, wbf):
    t = pl.program_id(1)

    @pl.when(t < nu_ref[0])
    def _():
        @pl.when(_new_expert(te_ref, t))
        def _():
            wbf[...] = w_ref[...].astype(BF16)

        gu = _dot(xs_ref[...], wbf[...]) + b_ref[...]
        tn = gu.shape[1]
        nxt = pltpu.roll(gu, tn - 1, 1)
        glu = jnp.minimum(gu, SWIGLU_LIMIT)
        lin = jnp.clip(nxt, -SWIGLU_LIMIT, SWIGLU_LIMIT)
        act = glu * _sigmoid(SWIGLU_ALPHA * glu) * (lin + 1.0)
        even = lax.broadcasted_iota(I32, act.shape, 1) % 2 == 0
        act = jnp.where(even, act, 0.0).astype(BF16)
        blk = 2 * LANES
        r = lax.broadcasted_iota(I32, (blk, LANES), 0)
        c = lax.broadcasted_iota(I32, (blk, LANES), 1)
        pick = jnp.where(r == 2 * c, 1.0, 0.0).astype(BF16)
        for j in range(tn // blk):
            act_ref[:, j * LANES:(j + 1) * LANES] = _dot(act[:, j * blk:(j + 1) * blk], pick).astype(act_ref.dtype)

    @pl.when(t >= nu_ref[0])
    def _():
        act_ref[...] = jnp.zeros(act_ref.shape, act_ref.dtype)


def _moe_up(xs, w_up, b_up, tile_expert, n_used, tm, n_tiles):
    n_experts, d, two_f = w_up.shape
    tn = min(MOE_UP_COLS, two_f)

    def tile(t, nu):
        return jnp.minimum(t, nu[0] - 1)
    return pl.pallas_call(
        _moe_up_kernel,
        out_shape=jax.ShapeDtypeStruct((n_tiles * tm, two_f // 2), BF16),
        grid_spec=pltpu.PrefetchScalarGridSpec(
            num_scalar_prefetch=2, grid=(two_f // tn, n_tiles),
            in_specs=[pl.BlockSpec((tm, d), lambda f, t, te, nu: (tile(t, nu), 0)),
                      pl.BlockSpec((None, d, tn), lambda f, t, te, nu: (te[tile(t, nu)], 0, f)),
                      pl.BlockSpec((None, 1, tn), lambda f, t, te, nu: (te[tile(t, nu)], 0, f))],
            out_specs=pl.BlockSpec((tm, tn // 2), lambda f, t, te, nu: (t, f)),
            scratch_shapes=[pltpu.VMEM((d, tn), BF16)]),
        compiler_params=_cparams("arbitrary", "arbitrary"),
        name="moe_up",
    )(tile_expert, n_used, xs, w_up, b_up.reshape(n_experts, 1, two_f))


def _moe_down_kernel(te_ref, nu_ref, act_ref, w_ref, b_ref, gate_ref, y_ref, wbf):
    t = pl.program_id(1)

    @pl.when(t < nu_ref[0])
    def _():
        @pl.when(_new_expert(te_ref, t))
        def _():
            wbf[...] = w_ref[...].astype(BF16)

        y = _dot(act_ref[...], wbf[...]) + b_ref[...]
        y_ref[...] = gate_ref[...] * y

    @pl.when(t >= nu_ref[0])
    def _():
        y_ref[...] = jnp.zeros(y_ref.shape, y_ref.dtype)


def _moe_down(act, w_down, b_down, row_gate, tile_expert, n_used, tm, n_tiles):
    n_experts, f, d = w_down.shape
    tn = min(MOE_DOWN_COLS, d)

    def tile(t, nu):
        return jnp.minimum(t, nu[0] - 1)
    return pl.pallas_call(
        _moe_down_kernel,
        out_shape=jax.ShapeDtypeStruct((n_tiles * tm, d), F32),
        grid_spec=pltpu.PrefetchScalarGridSpec(
            num_scalar_prefetch=2, grid=(d // tn, n_tiles),
            in_specs=[pl.BlockSpec((tm, f), lambda j, t, te, nu: (tile(t, nu), 0)),
                      pl.BlockSpec((None, f, tn), lambda j, t, te, nu: (te[tile(t, nu)], 0, j)),
                      pl.BlockSpec((None, 1, tn), lambda j, t, te, nu: (te[tile(t, nu)], 0, j)),
                      pl.BlockSpec((None, tm, 1), lambda j, t, te, nu: (tile(t, nu), 0, 0))],
            out_specs=pl.BlockSpec((tm, tn), lambda j, t, te, nu: (t, j)),
            scratch_shapes=[pltpu.VMEM((f, tn), BF16)]),
        compiler_params=_cparams("arbitrary", "arbitrary"),
        name="moe_down",
    )(tile_expert, n_used, act, w_down, b_down.reshape(n_experts, 1, d), row_gate.reshape(n_tiles, tm, 1))


def _combine_kernel(dest_ref, dest_next_ref, ys_hbm, x1_ref, g2_ref, npost_ref, y_ref, buf, sem):
    i = pl.program_id(0)
    n_steps = pl.num_programs(0)
    tm = x1_ref.shape[0]

    def issue(idx_ref, slot):
        def body(r, carry):
            for k in range(TOP_K):
                _row_copy(ys_hbm, idx_ref[0, r * TOP_K + k], buf.at[slot, k], r, sem.at[slot]).start()
            return carry
        lax.fori_loop(0, tm, body, 0, unroll=4)

    @pl.when(i == 0)
    def _():
        issue(dest_ref, 0)

    @pl.when(i + 1 < n_steps)
    def _():
        issue(dest_next_ref, (i + 1) % 2)

    slot = i % 2
    for k in range(TOP_K):
        pltpu.make_async_copy(ys_hbm.at[pl.ds(0, tm), :], buf.at[slot, k], sem.at[slot]).wait()
    moe = buf[slot, 0]
    for k in range(1, TOP_K):
        moe = moe + buf[slot, k]
    y_ref[...] = x1_ref[...] + g2_ref[...] * (_rms(moe) * npost_ref[...])


def _moe_combine(ys, dest, x1, mod, npost):
    n, d = x1.shape
    tm = min(COMBINE_TILE, n)
    steps = n // tm
    dest3 = dest.reshape(steps, 1, tm * TOP_K)
    row = pl.BlockSpec((tm, d), lambda i: (i, 0))
    return pl.pallas_call(
        _combine_kernel,
        out_shape=jax.ShapeDtypeStruct((n, d), F32),
        grid=(steps,),
        in_specs=[pl.BlockSpec((None, 1, tm * TOP_K), lambda i: (i, 0, 0), memory_space=pltpu.SMEM),
                  pl.BlockSpec((None, 1, tm * TOP_K), lambda i: (jnp.minimum(i + 1, steps - 1), 0, 0),
                               memory_space=pltpu.SMEM),
                  pl.BlockSpec(memory_space=pl.ANY), row, mod.spec(5, tm), pl.BlockSpec((1, d), lambda i: (0, 0))],
        out_specs=row,
        scratch_shapes=[pltpu.VMEM((2, TOP_K, tm, d), F32), pltpu.SemaphoreType.DMA((2,))],
        compiler_params=_cparams("arbitrary"),
        name="moe_combine",
    )(dest3, dest3, ys, x1, mod.array, npost)


def _layer(l, x_prompt, x_sample, c_prompt, c_sample, cache_k_fox, cache_v_fox, cache_logf_fox,
           cache_k_sb, cache_v_sb, page_table, w_ada, b_ada, norm_pre_mix, norm_post_mix,
           norm_pre_ffn, norm_post_ffn, w_in, b_forget, w_out, w_router, b_router,
           w_up, b_up, w_down, b_down):
    batch, seq, d = x_prompt.shape
    bs, n_new, _ = x_sample.shape
    n_heads = d // HEAD_DIM
    n_kv = n_heads // GROUP
    kv_w = n_kv * HEAD_DIM
    n_experts = w_router.shape[-1]
    pool, page = cache_k_fox.shape[1], cache_k_fox.shape[2]
    np_, ns = batch * seq, bs * n_new

    f0 = d + 2 * kv_w
    w_l = w_in[l]
    w_main = jnp.concatenate([w_l[:, :f0], w_l[:, f0 + n_heads:]], axis=1).astype(BF16)
    wf_pad = jnp.pad(w_l[:, f0:f0 + n_heads], ((0, 0), (0, LANES - n_heads))).astype(BF16)
    bf_pad = jnp.pad(b_forget[l], (0, LANES - n_heads)).reshape(1, LANES)
    wo_bf = w_out[l].astype(BF16)
    wr_pad = jnp.pad(w_router[l], ((0, 0), (0, LANES - n_experts))).astype(BF16)
    br_pad = jnp.pad(b_router[l], (0, LANES - n_experts), constant_values=NEG_INF).reshape(1, LANES)
    norms = [n_[l].reshape(1, d) for n_ in (norm_pre_mix, norm_post_mix, norm_pre_ffn, norm_post_ffn)]

    n_c = batch + bs
    c_all = jnp.concatenate([c_prompt, c_sample, jnp.zeros((-n_c % 8, d), F32)], axis=0)
    mod_all = _modulation(c_all, w_ada[l], b_ada[l])
    mod_p = _prompt_mod(mod_all[:batch], seq, d)
    mod_s = _sample_mod(mod_all[batch:n_c], n_new, d)

    xp = x_prompt.reshape(np_, d)
    xs = x_sample.reshape(ns, d)
    scale = HEAD_DIM ** -0.5
    col = {"qa": 0, "ka": d, "va": d + kv_w, "qb": f0, "kb": f0 + d, "vb": f0 + d + kv_w, "gate": f0 + d + 2 * kv_w}

    def mixer_inputs(x, mod):
        h, logf = _prenorm(x, mod, norms[0], wf_pad, bf_pad, n_heads)
        qa, = _project(h, w_main, col["qa"], d, (BF16,), scale)
        qb, = _project(h, w_main, col["qb"], d, (BF16,), scale)
        ka = _project(h, w_main, col["ka"], kv_w, (F32, BF16))
        va = _project(h, w_main, col["va"], kv_w, (F32, BF16))
        kb = _project(h, w_main, col["kb"], kv_w, (F32, BF16))
        vb = _project(h, w_main, col["vb"], kv_w, (F32, BF16))
        gates, = _project(h, w_main, col["gate"], 2 * d, (F32,))
        return logf, qa, qb, ka, va, kb, vb, gates

    logf_p, qa, qb, ka, va, kb, vb, gates = mixer_inputs(xp, mod_p)
    f_rows = _cumsum_rows(logf_p.reshape(batch, seq, n_heads).transpose(0, 2, 1))
    o_p = _prompt_attention(qa, qb, ka[1], va[1], kb[1], vb[1], f_rows, gates, batch, seq, n_kv)
    state_p = (ka[0].reshape(batch, seq, n_kv, HEAD_DIM), va[0].reshape(batch, seq, n_kv, HEAD_DIM),
               logf_p.reshape(batch, seq, n_heads),
               kb[0].reshape(batch, seq, n_kv, HEAD_DIM), vb[0].reshape(batch, seq, n_kv, HEAD_DIM))

    logf_s, qa, qb, ka, va, kb, vb, gates = mixer_inputs(xs, mod_s)
    rows = n_new * n_heads
    pad_new = lambda a: jnp.pad(a.reshape(bs, n_new, kv_w), ((0, 0), (0, -n_new % 8), (0, 0)))
    lf_s = logf_s.reshape(bs, n_new, n_heads)
    bias_new = jnp.pad(-jnp.cumsum(lf_s, axis=1).transpose(0, 2, 1), ((0, 0), (0, 0), (0, page - n_new)))
    o_s = _sample_attention(
        qa.reshape(bs, rows, HEAD_DIM), qb.reshape(bs, rows, HEAD_DIM),
        pad_new(ka[0]), pad_new(va[0]), pad_new(kb[0]), pad_new(vb[0]), bias_new,
        gates.reshape(bs, 2 * rows, HEAD_DIM), cache_logf_fox[l].transpose(0, 2, 1),
        *(c[l].reshape(pool, page * n_kv, HEAD_DIM) for c in (cache_k_fox, cache_v_fox, cache_k_sb, cache_v_sb)),
        page_table, n_heads, n_kv, n_new).reshape(ns, d)
    state_s = (ka[0].reshape(bs, n_new, n_kv, HEAD_DIM), va[0].reshape(bs, n_new, n_kv, HEAD_DIM), lf_s,
               kb[0].reshape(bs, n_new, n_kv, HEAD_DIM), vb[0].reshape(bs, n_new, n_kv, HEAD_DIM))

    x1_p, h2_p, ti_p, tg_p = _post_attention(o_p, wo_bf, xp, mod_p, norms[1], norms[2], wr_pad, br_pad)
    x1_s, h2_s, ti_s, tg_s = _post_attention(o_s, wo_bf, xs, mod_s, norms[1], norms[2], wr_pad, br_pad)
    h2 = jnp.concatenate([h2_p, h2_s], axis=0)
    top_i = jnp.concatenate([ti_p[:, :TOP_K], ti_s[:, :TOP_K]], axis=0)
    top_g = jnp.concatenate([tg_p[:, :TOP_K], tg_s[:, :TOP_K]], axis=0)

    n_tok = np_ + ns
    tm = MOE_TILE
    n_tiles = (n_tok * TOP_K) // tm + n_experts
    dest, tile_expert, n_used, row_token, row_gate = _route_plan(top_i, top_g, n_experts, tm, n_tiles)
    xsort = _moe_gather(h2, row_token, n_used, tm, n_tiles)
    act = _moe_up(xsort, w_up[l], b_up[l], tile_expert, n_used, tm, n_tiles)
    ys = _moe_down(act, w_down[l], b_down[l], row_gate, tile_expert, n_used, tm, n_tiles)
    y_p = _moe_combine(ys, dest[:np_], x1_p, mod_p, norms[3])
    y_s = _moe_combine(ys, dest[np_:], x1_s, mod_s, norms[3])
    return y_p.reshape(batch, seq, d), y_s.reshape(bs, n_new, d), state_p, state_s


def kernel(x_prompt, x_sample, c_prompt, c_sample, cache_k_fox, cache_v_fox, cache_logf_fox, cache_k_sb, cache_v_sb, page_table, w_ada, b_ada, norm_pre_mix, norm_post_mix, norm_pre_ffn, norm_post_ffn, w_in, b_forget, w_out, w_router, b_router, w_up, b_up, w_down, b_down):
    depth = w_in.shape[0]
    xp, xs = x_prompt, x_sample
    p_states, s_states = [], []
    for l in range(depth):
        xp, xs, sp, ss = _layer(l, xp, xs, c_prompt, c_sample, cache_k_fox, cache_v_fox, cache_logf_fox,
                                cache_k_sb, cache_v_sb, page_table, w_ada, b_ada, norm_pre_mix, norm_post_mix,
                                norm_pre_ffn, norm_post_ffn, w_in, b_forget, w_out, w_router, b_router,
                                w_up, b_up, w_down, b_down)
        p_states.append(sp)
        s_states.append(ss)
    k_fox_p, v_fox_p, logf_fox_p, k_sb_p, v_sb_p = (jnp.stack(t) for t in zip(*p_states))
    k_fox_s, v_fox_s, logf_fox_s, k_sb_s, v_sb_s = (jnp.stack(t) for t in zip(*s_states))
    return (xp, xs, k_fox_p, v_fox_p, logf_fox_p, k_sb_p, v_sb_p,
            k_fox_s, v_fox_s, logf_fox_s, k_sb_s, v_sb_s)
```

```python
import functools

import jax
import jax.numpy as jnp
from jax import lax
from jax.experimental import pallas as pl
from jax.experimental.pallas import tpu as pltpu

F32 = jnp.float32
BF16 = jnp.bfloat16
I32 = jnp.int32

HEAD_DIM = 128
GROUP = 2
TOP_K = 4
SWIGLU_LIMIT = 7.0
SWIGLU_ALPHA = 1.702
RMS_EPS = 1e-6
NEG_INF = -1e30
EXP_UNDERFLOW = -104.0
LANES = 128
VMEM_LIMIT_BYTES = 48 * 1024 * 1024

ROW_TILE = 512
PROJ_COLS = 1024
ATTN_TILE = 256
ATTN_KEY_BLOCKS = 2
POST_TILE = 256
MOE_TILE = 256
MOE_UP_COLS = 1024
MOE_DOWN_COLS = 1024
COMBINE_TILE = 128


def _cparams(*sem):
    return pltpu.CompilerParams(dimension_semantics=sem, vmem_limit_bytes=VMEM_LIMIT_BYTES)


def _sigmoid(x):
    return 1.0 / (1.0 + jnp.exp(-x))


def _log_sigmoid(x):
    return jnp.minimum(x, 0.0) - jnp.log1p(jnp.exp(-jnp.abs(x)))


def _rms(x):
    return x * lax.rsqrt(jnp.mean(x * x, axis=-1, keepdims=True) + RMS_EPS)


def _split_bf16(x, parts):
    out = []
    for _ in range(parts - 1):
        p = x.astype(BF16)
        out.append(p)
        x = x - p.astype(F32)
    out.append(x.astype(BF16))
    return out


def _dot_nt(a, b):
    return lax.dot_general(a, b, (((1,), (1,)), ((), ())), preferred_element_type=F32)


def _dot(a, b):
    return jnp.dot(a, b, preferred_element_type=F32)


def _strict_lower_ones(n):
    r = lax.broadcasted_iota(I32, (n, n), 0)
    c = lax.broadcasted_iota(I32, (n, n), 1)
    return jnp.where(r > c, 1.0, 0.0).astype(BF16)


def _ada_kernel(c_ref, w_ref, b_ref, o_ref):
    c = c_ref[...]
    s = (c * _sigmoid(c)).astype(BF16)
    o_ref[...] = _dot(s, w_ref[...].astype(BF16)) + b_ref[...]


def _modulation(c_all, w_ada, b_ada):
    m, d = c_all.shape
    n6 = w_ada.shape[1]
    tn = PROJ_COLS
    return pl.pallas_call(
        _ada_kernel,
        out_shape=jax.ShapeDtypeStruct((m, n6), F32),
        grid=(n6 // tn,),
        in_specs=[pl.BlockSpec((m, d), lambda j: (0, 0)),
                  pl.BlockSpec((d, tn), lambda j: (0, j)),
                  pl.BlockSpec((1, tn), lambda j: (0, j))],
        out_specs=pl.BlockSpec((m, tn), lambda j: (0, j)),
        compiler_params=_cparams("arbitrary"),
        name="ada",
    )(c_all, w_ada, b_ada.reshape(1, n6))


class _Mod:
    def __init__(self, array, spec_fn):
        self.array = array
        self.spec = spec_fn


def _prompt_mod(mod_p, seq, d):
    arr = mod_p.reshape(mod_p.shape[0], 1, 6 * d)

    def spec(k, tm):
        per_seq = seq // tm
        return pl.BlockSpec((None, 1, d), lambda i, *_: (i // per_seq, 0, k))
    return _Mod(arr, spec)


def _sample_mod(mod_s, t, d):
    arr = jnp.repeat(mod_s, t, axis=0)

    def spec(k, tm):
        return pl.BlockSpec((tm, d), lambda i, *_: (i, k))
    return _Mod(arr, spec)


def _prenorm_kernel(x_ref, g_ref, sh_ref, sc_ref, wf_ref, bf_ref, h_ref, lf_ref):
    h = _rms(x_ref[...]) * g_ref[...] * (1.0 + sc_ref[...]) + sh_ref[...]
    hb = h.astype(BF16)
    h_ref[...] = hb
    fa = _dot(hb, wf_ref[...]) + bf_ref[...]
    lf_ref[...] = _log_sigmoid(fa)[:, :lf_ref.shape[1]]


def _prenorm(x, mod, g, wf_pad, bf_pad, n_heads):
    n, d = x.shape
    tm = min(ROW_TILE, n)
    return pl.pallas_call(
        _prenorm_kernel,
        out_shape=(jax.ShapeDtypeStruct((n, d), BF16), jax.ShapeDtypeStruct((n, n_heads), F32)),
        grid=(n // tm,),
        in_specs=[pl.BlockSpec((tm, d), lambda i: (i, 0)),
                  pl.BlockSpec((1, d), lambda i: (0, 0)),
                  mod.spec(0, tm), mod.spec(1, tm),
                  pl.BlockSpec((d, LANES), lambda i: (0, 0)),
                  pl.BlockSpec((1, LANES), lambda i: (0, 0))],
        out_specs=(pl.BlockSpec((tm, d), lambda i: (i, 0)),
                   pl.BlockSpec((tm, n_heads), lambda i: (i, 0))),
        compiler_params=_cparams("arbitrary"),
        name="prenorm",
    )(x, g, mod.array, mod.array, wf_pad, bf_pad)


def _proj_kernel(h_ref, w_ref, *o_refs, scale):
    acc = _dot(h_ref[...], w_ref[...])
    if scale is not None:
        acc = acc * scale
    for o_ref in o_refs:
        o_ref[...] = acc.astype(o_ref.dtype)


def _project(h, w_bf, col0, width, out_dtypes, scale=None):
    n, d = h.shape
    tm = min(ROW_TILE, n)
    tn = min(PROJ_COLS, width)
    cb0 = col0 // tn
    outs = pl.pallas_call(
        functools.partial(_proj_kernel, scale=scale),
        out_shape=tuple(jax.ShapeDtypeStruct((n, width), dt) for dt in out_dtypes),
        grid=(width // tn, n // tm),
        in_specs=[pl.BlockSpec((tm, d), lambda j, i: (i, 0)),
                  pl.BlockSpec((d, tn), lambda j, i: (0, cb0 + j))],
        out_specs=tuple(pl.BlockSpec((tm, tn), lambda j, i: (i, j)) for _ in out_dtypes),
        compiler_params=_cparams("arbitrary", "arbitrary"),
        name="proj",
    )(h, w_bf)
    return outs


def _cumsum_kernel(lf_ref, o_ref):
    nh, s = lf_ref.shape
    blk = LANES
    r = lax.broadcasted_iota(I32, (blk, blk), 0)
    c = lax.broadcasted_iota(I32, (blk, blk), 1)
    upper = jnp.where(r <= c, 1.0, 0.0).astype(BF16)

    def body(i, carry):
        off = pl.multiple_of(i * blk, blk)
        x = lf_ref[:, pl.ds(off, blk)]
        cs = carry
        for p in _split_bf16(x, 3):
            cs = cs + _dot(p, upper)
        o_ref[:, pl.ds(off, blk)] = cs
        return cs[:, blk - 1:blk]

    lax.fori_loop(0, s // blk, body, jnp.zeros((nh, 1), F32))


def _cumsum_rows(lf_t):
    b, nh, s = lf_t.shape
    return pl.pallas_call(
        _cumsum_kernel,
        out_shape=jax.ShapeDtypeStruct((b, nh, s), F32),
        grid=(b,),
        in_specs=[pl.BlockSpec((None, nh, s), lambda i: (i, 0, 0))],
        out_specs=pl.BlockSpec((None, nh, s), lambda i: (i, 0, 0)),
        compiler_params=_cparams("arbitrary"),
        name="cumsum",
    )(lf_t)


def _prompt_attn_kernel(qa_ref, qb_ref, ka_ref, vat_ref, kb_ref, vbt_ref, ga_ref, gb_ref, o_ref, *, tk):
    tq = qa_ref.shape[0]
    hd = HEAD_DIM
    qi = pl.program_id(2)
    q0 = qi * tq
    w = GROUP * tq
    lane = lax.broadcasted_iota(I32, (tq, hd), 1)
    picks = [jnp.where((lane >= 3 * r) & (lane < 3 * r + 3), 1.0, 0.0).astype(BF16) for r in range(GROUP)]
    qa = jnp.concatenate([jnp.concatenate([qa_ref[:, r * hd:(r + 1) * hd], picks[r]], axis=1)
                          for r in range(GROUP)], axis=0)
    qb = jnp.concatenate([qb_ref[:, r * hd:(r + 1) * hd] for r in range(GROUP)], axis=0)

    def key_before_query(k0, n_keys, strict):
        key_pos = k0 + lax.broadcasted_iota(I32, (n_keys, w), 0)
        query_pos = q0 + lax.broadcasted_iota(I32, (n_keys, w), 1) % tq
        return key_pos < query_pos if strict else key_pos <= query_pos

    def fox_block(k0, n_keys, carry, diag):
        m, l, acc = carry
        s = _dot_nt(ka_ref[pl.ds(k0, n_keys), :], qa)
        if diag:
            s = jnp.where(key_before_query(k0, n_keys, False), s, NEG_INF)
        m_new = jnp.maximum(m, jnp.max(s, axis=0, keepdims=True))
        alpha = jnp.exp(m - m_new)
        p = jnp.exp(s - m_new)
        l = alpha * l + jnp.sum(p, axis=0, keepdims=True)
        acc = alpha * acc + _dot(vat_ref[:, pl.ds(k0, n_keys)], p.astype(BF16))
        return m_new, l, acc

    per_big = tk // tq
    carry = (jnp.full((1, w), NEG_INF, F32), jnp.zeros((1, w), F32), jnp.zeros((hd, w), F32))
    carry = lax.fori_loop(
        0, qi // per_big, lambda j, cr: fox_block(pl.multiple_of(j * tk, tk), tk, cr, False), carry)
    rest0 = (qi // per_big) * tk
    carry = lax.fori_loop(
        0, qi % per_big, lambda j, cr: fox_block(pl.multiple_of(rest0 + j * tq, tq), tq, cr, False), carry)
    _, l, acc = fox_block(pl.multiple_of(q0, tq), tq, carry, True)
    oa = acc / l

    r_ = lax.broadcasted_iota(I32, (tq, tq), 0)
    c_ = lax.broadcasted_iota(I32, (tq, tq), 1)
    later_keys = jnp.where(c_ > r_, 1.0, 0.0).astype(BF16)

    def sb_block(kj, c, acc, diag):
        k0 = pl.multiple_of(kj * tq, tq)
        z = _dot_nt(kb_ref[pl.ds(k0, tq), :], qb)
        t = jnp.log1p(jnp.exp(-jnp.abs(z)))
        lneg = -jnp.maximum(z, 0.0) - t
        lpos = jnp.minimum(z, 0.0) - t
        if diag:
            visible = key_before_query(k0, tq, True)
            lneg = jnp.where(visible, lneg, 0.0)
        suf = c
        for piece in _split_bf16(lneg, 2):
            suf = suf + _dot(later_keys, piece)
        wgt = jnp.exp(lpos + suf)
        if diag:
            wgt = jnp.where(visible, wgt, 0.0)
        acc = acc + _dot(vbt_ref[:, pl.ds(k0, tq)], wgt.astype(BF16))
        c = c + jnp.sum(lneg, axis=0, keepdims=True)
        return c, acc

    c, acc = sb_block(qi, jnp.zeros((1, w), F32), jnp.zeros((hd, w), F32), True)

    def cond(st):
        return jnp.logical_and(st[0] >= 0, jnp.max(st[1]) > EXP_UNDERFLOW)

    def body(st):
        c2, acc2 = sb_block(st[0], st[1], st[2], False)
        return st[0] - 1, c2, acc2

    ob = lax.while_loop(cond, body, (qi - 1, c, acc))[2]

    for r in range(GROUP):
        ga = ga_ref[:, r * hd:(r + 1) * hd]
        gb = gb_ref[:, r * hd:(r + 1) * hd]
        oa_r = oa[:, r * tq:(r + 1) * tq].T
        ob_r = ob[:, r * tq:(r + 1) * tq].T
        o_ref[:, r * hd:(r + 1) * hd] = (_sigmoid(ga) * oa_r + _sigmoid(gb) * ob_r).astype(o_ref.dtype)


def _prompt_attention(qa, qb, ka_aug, va_t, kb, vb_t, gates, batch, seq, n_kv):
    n, dq = qa.shape
    tq = min(ATTN_TILE, seq)
    tk = ATTN_KEY_BLOCKS * tq
    nq = seq // tq
    gw = GROUP * HEAD_DIM
    qspec = pl.BlockSpec((tq, gw), lambda b, g, i: (b * nq + i, g))
    vspec = pl.BlockSpec((None, HEAD_DIM, seq), lambda b, g, i: (b, g, 0))
    return pl.pallas_call(
        functools.partial(_prompt_attn_kernel, tk=tk),
        out_shape=jax.ShapeDtypeStruct((n, dq), BF16),
        grid=(batch, n_kv, nq),
        in_specs=[qspec, qspec,
                  pl.BlockSpec((seq, 2 * HEAD_DIM), lambda b, g, i: (b, g)), vspec,
                  pl.BlockSpec((seq, HEAD_DIM), lambda b, g, i: (b, g)), vspec,
                  pl.BlockSpec((tq, gw), lambda b, g, i: (b * nq + i, g)),
                  pl.BlockSpec((tq, gw), lambda b, g, i: (b * nq + i, n_kv + g))],
        out_specs=qspec,
        compiler_params=_cparams("arbitrary", "arbitrary", "arbitrary"),
        name="prompt_attn",
    )(qa, qb, ka_aug, va_t, kb, vb_t, gates, gates)


def _block_diag_queries(q, n_heads, n_kv):
    rows = q.shape[0]
    kv_of_row = (lax.broadcasted_iota(I32, (rows, 1), 0) % n_heads) // GROUP
    blocks = [jnp.where(kv_of_row == g, q, jnp.zeros_like(q)) for g in range(n_kv)]
    return jnp.concatenate(blocks, axis=1), kv_of_row


def _own_head_block(pv, kv_of_row, n_kv):
    out = jnp.zeros((pv.shape[0], HEAD_DIM), F32)
    for g in range(n_kv):
        out = out + jnp.where(kv_of_row == g, pv[:, g * HEAD_DIM:(g + 1) * HEAD_DIM], 0.0)
    return out


def _page_copies(pt_ref, b, p, slot, kc_ref, vc_ref, kbuf, vbuf, sem):
    page = pt_ref[b, p]
    return (pltpu.make_async_copy(kc_ref.at[page], kbuf.at[slot], sem.at[0, slot]),
            pltpu.make_async_copy(vc_ref.at[page], vbuf.at[slot], sem.at[1, slot]))


def _load_page(buf, slot, n_kv):
    tokens = buf.shape[1] // n_kv
    heads = [buf[slot, pl.ds(g, tokens, stride=n_kv), :].astype(BF16) for g in range(n_kv)]
    return jnp.concatenate(heads, axis=1)


def _sample_fox_kernel(pt_ref, q_ref, kn_ref, vn_ref, bn_ref, *rest, n_pages, n_heads, n_kv, n_new):
    lf_refs = rest[:n_pages]
    kc_ref, vc_ref, o_ref, kbuf, vbuf, bias_s, sem = rest[n_pages:]
    b = pl.program_id(0)
    cols = kbuf.shape[1]
    page = cols // n_kv
    rows = q_ref.shape[0]
    last = n_pages - 1

    def copies(p, slot):
        return _page_copies(pt_ref, b, p, slot, kc_ref, vc_ref, kbuf, vbuf, sem)

    for cp in copies(last, last % 2):
        cp.start()

    x = jnp.concatenate([lf_refs[p][...] for p in range(n_pages)], axis=0)
    key_of_col = lax.broadcasted_iota(I32, (page, cols), 1) // n_kv
    spread = jnp.where(lax.broadcasted_iota(I32, (page, cols), 0) > key_of_col, 1.0, 0.0).astype(BF16)
    within = jnp.zeros((x.shape[0], cols), F32)
    for piece in _split_bf16(x, 3):
        within = within + _dot(piece, spread)
    totals = jnp.sum(x, axis=1, keepdims=True)
    own = (lax.broadcasted_iota(I32, (n_heads, cols), 1) % n_kv) == (lax.broadcasted_iota(I32, (n_heads, cols), 0) // GROUP)
    other_head = jnp.where(own, 0.0, NEG_INF)
    later = jnp.zeros((n_heads, 1), F32)
    for p in range(last, -1, -1):
        bias_s[p] = within[p * n_heads:(p + 1) * n_heads] + later + other_head
        later = later + totals[p * n_heads:(p + 1) * n_heads]

    q = q_ref[...]

    def attend(kp, vp, bias, mask, carry):
        m, l, acc = carry
        s = _dot_nt(q, kp) + bias
        if mask is not None:
            s = jnp.where(mask, s, NEG_INF)
        m_new = jnp.maximum(m, jnp.max(s, axis=1, keepdims=True))
        alpha = jnp.exp(m - m_new)
        p = jnp.exp(s - m_new)
        l = alpha * l + jnp.sum(p, axis=1, keepdims=True)
        acc = alpha * acc + _dot(p.astype(BF16), vp)
        return m_new, l, acc

    init = (jnp.full((rows, 1), NEG_INF, F32), jnp.zeros((rows, 1), F32), jnp.zeros((rows, HEAD_DIM), F32))
    new_cols = kn_ref.shape[0]
    t_of_row = lax.broadcasted_iota(I32, (rows, new_cols), 0) // n_heads
    j_of_col = lax.broadcasted_iota(I32, (rows, new_cols), 1) // n_kv
    carry = attend(kn_ref[...].astype(BF16), vn_ref[...].astype(BF16),
                   jnp.concatenate([bn_ref[...]] * n_new, axis=0), j_of_col <= t_of_row, init)

    def body(i, carry):
        p = last - i
        slot = p % 2
        for cp in copies(p, slot):
            cp.wait()

        @pl.when(p > 0)
        def _():
            for cp in copies(p - 1, 1 - slot):
                cp.start()

        bias = jnp.concatenate([bias_s[p]] * n_new, axis=0)
        return attend(kbuf[slot].astype(BF16), vbuf[slot].astype(BF16), bias, None, carry)

    _, l, acc = lax.fori_loop(0, n_pages, body, carry)
    o_ref[...] = acc / l


def _sample_sb_kernel(pt_ref, q_ref, kn_ref, vn_ref, oa_ref, gate_ref, kc_ref, vc_ref, o_ref,
                      kbuf, vbuf, sem, *, n_pages, n_heads, n_kv, n_new):
    b = pl.program_id(0)
    page = kbuf.shape[1] // n_kv
    rows = q_ref.shape[0]
    last = n_pages - 1

    def copies(p, slot):
        return _page_copies(pt_ref, b, p, slot, kc_ref, vc_ref, kbuf, vbuf, sem)

    for cp in copies(last, last % 2):
        cp.start()

    qbd, kv_of_row = _block_diag_queries(q_ref[...], n_heads, n_kv)
    t_of_row = lax.broadcasted_iota(I32, (rows, page), 0) // n_heads
    lane = lax.broadcasted_iota(I32, (rows, page), 1)
    ones = _strict_lower_ones(page)

    def attend(kp, vp, mask, c, acc):
        z = _dot_nt(qbd, kp)
        t = jnp.log1p(jnp.exp(-jnp.abs(z)))
        lneg = -jnp.maximum(z, 0.0) - t
        lpos = jnp.minimum(z, 0.0) - t
        if mask is not None:
            lneg = jnp.where(mask, lneg, 0.0)
        suf = c
        for piece in _split_bf16(lneg, 2):
            suf = suf + _dot(piece, ones)
        w = jnp.exp(lpos + suf)
        if mask is not None:
            w = jnp.where(mask, w, 0.0)
        acc = acc + _own_head_block(_dot(w.astype(BF16), vp), kv_of_row, n_kv)
        c = c + jnp.sum(lneg, axis=1, keepdims=True)
        return c, acc

    def pad_page(ref):
        new = ref[...]
        return jnp.concatenate([new, jnp.zeros((page - new.shape[0], new.shape[1]), F32)], axis=0).astype(BF16)

    c, acc = attend(pad_page(kn_ref), pad_page(vn_ref), lane < t_of_row,
                    jnp.zeros((rows, 1), F32), jnp.zeros((rows, HEAD_DIM), F32))

    def cond(st):
        return jnp.logical_and(st[0] >= 0, jnp.max(st[1]) > EXP_UNDERFLOW)

    def body(st):
        p = st[0]
        slot = p % 2
        for cp in copies(p, slot):
            cp.wait()

        @pl.when(p > 0)
        def _():
            for cp in copies(p - 1, 1 - slot):
                cp.start()

        c2, acc2 = attend(_load_page(kbuf, slot, n_kv), _load_page(vbuf, slot, n_kv), None, st[1], st[2])
        return p - 1, c2, acc2

    p_end, _, ob = lax.while_loop(cond, body, (jnp.int32(last), c, acc))

    @pl.when(p_end >= 0)
    def _():
        for cp in copies(jnp.maximum(p_end, 0), jnp.maximum(p_end, 0) % 2):
            cp.wait()

    ga = jnp.concatenate([gate_ref[t * 2 * n_heads:t * 2 * n_heads + n_heads, :] for t in range(n_new)], axis=0)
    gb = jnp.concatenate([gate_ref[t * 2 * n_heads + n_heads:(t + 1) * 2 * n_heads, :] for t in range(n_new)], axis=0)
    o_ref[...] = (_sigmoid(ga) * oa_ref[...] + _sigmoid(gb) * ob).astype(o_ref.dtype)


def _sample_attention(qa, qb, ka_flat, va_flat, kb_new, vb_new, bias_new, gates, lf_pool_t,
                      ck_fox, cv_fox, ck_sb, cv_sb, page_table, n_heads, n_kv, n_new):
    bs, n_pages = page_table.shape
    pool, page_rows, _ = ck_fox.shape
    page = page_rows // n_kv
    w = n_kv * HEAD_DIM
    rows = n_new * n_heads
    seq_rows = pl.BlockSpec((None, rows, HEAD_DIM), lambda b, pt: (b, 0, 0))
    new_flat = pl.BlockSpec((None, ka_flat.shape[1], HEAD_DIM), lambda b, pt: (b, 0, 0))
    new_kv = pl.BlockSpec((None, kb_new.shape[1], w), lambda b, pt: (b, 0, 0))
    any_spec = pl.BlockSpec(memory_space=pl.ANY)
    scratch = [pltpu.VMEM((2, page_rows, HEAD_DIM), F32), pltpu.VMEM((2, page_rows, HEAD_DIM), F32)]

    lf_specs = [pl.BlockSpec((None, n_heads, page), functools.partial(lambda b, pt, p: (pt[b, p], 0, 0), p=p))
                for p in range(n_pages)]
    oa = pl.pallas_call(
        functools.partial(_sample_fox_kernel, n_pages=n_pages, n_heads=n_heads, n_kv=n_kv, n_new=n_new),
        out_shape=jax.ShapeDtypeStruct((bs, rows, HEAD_DIM), F32),
        grid_spec=pltpu.PrefetchScalarGridSpec(
            num_scalar_prefetch=1, grid=(bs,),
            in_specs=[seq_rows, new_flat, new_flat,
                      pl.BlockSpec((None, n_heads, bias_new.shape[2]), lambda b, pt: (b, 0, 0))]
                     + lf_specs + [any_spec, any_spec],
            out_specs=seq_rows,
            scratch_shapes=scratch + [pltpu.VMEM((n_pages, n_heads, page_rows), F32),
                                      pltpu.SemaphoreType.DMA((2, 2))]),
        compiler_params=_cparams("arbitrary"),
        name="sample_fox",
    )(page_table, qa, ka_flat, va_flat, bias_new, *([lf_pool_t] * n_pages), ck_fox, cv_fox)

    return pl.pallas_call(
        functools.partial(_sample_sb_kernel, n_pages=n_pages, n_heads=n_heads, n_kv=n_kv, n_new=n_new),
        out_shape=jax.ShapeDtypeStruct((bs, rows, HEAD_DIM), BF16),
        grid_spec=pltpu.PrefetchScalarGridSpec(
            num_scalar_prefetch=1, grid=(bs,),
            in_specs=[seq_rows, new_kv, new_kv, seq_rows,
                      pl.BlockSpec((None, 2 * rows, HEAD_DIM), lambda b, pt: (b, 0, 0)), any_spec, any_spec],
            out_specs=seq_rows,
            scratch_shapes=scratch + [pltpu.SemaphoreType.DMA((2, 2))]),
        compiler_params=_cparams("arbitrary"),
        name="sample_sb",
    )(page_table, qb, kb_new, vb_new, oa, gates, ck_sb, cv_sb)


def _post_kernel(o_ref, wo_ref, x_ref, g1_ref, sh2_ref, sc2_ref, npm_ref, npf_ref, wr_ref, br_ref,
                 x1_ref, h2_ref, ti_ref, tg_ref):
    m = _dot(o_ref[...], wo_ref[...])
    x1 = x_ref[...] + g1_ref[...] * (_rms(m) * npm_ref[...])
    x1_ref[...] = x1
    h2 = _rms(x1) * npf_ref[...] * (1.0 + sc2_ref[...]) + sh2_ref[...]
    h2_ref[...] = h2
    logits = _dot(h2.astype(BF16), wr_ref[...]) + br_ref[...]
    lane = lax.broadcasted_iota(I32, logits.shape, 1)
    lane_f = lane.astype(F32)
    work = logits
    vals, ids = [], []
    for _ in range(TOP_K):
        mk = jnp.max(work, axis=1, keepdims=True)
        ik = jnp.min(jnp.where(work == mk, lane_f, float(LANES)), axis=1, keepdims=True)
        vals.append(mk)
        ids.append(ik)
        work = jnp.where(lane_f == ik, -jnp.inf, work)
    es = [jnp.exp(v - vals[0]) for v in vals]
    den = es[0]
    for e in es[1:]:
        den = den + e
    ti = jnp.zeros(logits.shape, F32)
    tg = jnp.zeros(logits.shape, F32)
    for k in range(TOP_K):
        ti = jnp.where(lane == k, ids[k], ti)
        tg = jnp.where(lane == k, es[k] / den, tg)
    ti_ref[...] = ti.astype(I32)
    tg_ref[...] = tg


def _post_attention(o, wo_bf, x, mod, npm, npf, wr_pad, br_pad):
    n, d = x.shape
    tm = min(POST_TILE, n)
    row = pl.BlockSpec((tm, d), lambda i: (i, 0))
    vec = pl.BlockSpec((1, d), lambda i: (0, 0))
    small = pl.BlockSpec((tm, LANES), lambda i: (i, 0))
    return pl.pallas_call(
        _post_kernel,
        out_shape=(jax.ShapeDtypeStruct((n, d), F32), jax.ShapeDtypeStruct((n, d), F32),
                   jax.ShapeDtypeStruct((n, LANES), I32), jax.ShapeDtypeStruct((n, LANES), F32)),
        grid=(n // tm,),
        in_specs=[row, pl.BlockSpec((d, d), lambda i: (0, 0)), row,
                  mod.spec(2, tm), mod.spec(3, tm), mod.spec(4, tm), vec, vec,
                  pl.BlockSpec((d, LANES), lambda i: (0, 0)), pl.BlockSpec((1, LANES), lambda i: (0, 0))],
        out_specs=(row, row, small, small),
        compiler_params=_cparams("arbitrary"),
        name="post_attn",
    )(o, wo_bf, x, mod.array, mod.array, mod.array, npm, npf, wr_pad, br_pad)


def _route_kernel(ti_ref, dest_ref, cnt_ref, seen, base, *, tm):
    ph = pl.program_id(0)
    i = pl.program_id(1)
    tr = ti_ref.shape[0]
    lane = lax.broadcasted_iota(I32, (tr, LANES), 1)
    ti = ti_ref[...]
    hits = [lane == ti[:, k:k + 1] for k in range(TOP_K)]
    member = jnp.zeros((tr, LANES), F32)
    for hit in hits:
        member = member + jnp.where(hit, 1.0, 0.0)
    tile_counts = jnp.sum(member, axis=0, keepdims=True)

    @pl.when(jnp.logical_and(ph == 0, i == 0))
    def _():
        seen[...] = jnp.zeros(seen.shape, F32)

    @pl.when(ph == 0)
    def _():
        seen[...] = seen[...] + tile_counts
        dest_ref[...] = jnp.zeros(dest_ref.shape, I32)

    @pl.when(jnp.logical_and(ph == 1, i == 0))
    def _():
        counts = seen[...]
        cnt_ref[...] = jnp.broadcast_to(counts, cnt_ref.shape)
        tiles = jnp.floor((counts + (tm - 1)) * (1.0 / tm))
        r = lax.broadcasted_iota(I32, (LANES, LANES), 0)
        c = lax.broadcasted_iota(I32, (LANES, LANES), 1)
        before = jnp.where(r < c, 1.0, 0.0).astype(BF16)
        first_tile = jnp.zeros((1, LANES), F32)
        for piece in _split_bf16(jnp.broadcast_to(tiles, (8, LANES)), 2):
            first_tile = first_tile + _dot(piece, before)[0:1]
        base[...] = first_tile * tm
        seen[...] = jnp.zeros(seen.shape, F32)

    @pl.when(ph == 1)
    def _():
        r = lax.broadcasted_iota(I32, (tr, tr), 0)
        c = lax.broadcasted_iota(I32, (tr, tr), 1)
        earlier = jnp.where(c < r, 1.0, 0.0).astype(BF16)
        rank = _dot(earlier, member.astype(BF16)) + (base[...] + seen[...])
        dest = jnp.zeros((tr, LANES), F32)
        for k, hit in enumerate(hits):
            dest = jnp.where(lane == k, jnp.sum(jnp.where(hit, rank, 0.0), axis=1, keepdims=True), dest)
        dest_ref[...] = dest.astype(I32)
        seen[...] = seen[...] + tile_counts


def _route(top_i, tm):
    n = top_i.shape[0]
    tr = next(t for t in (256, 128, 64, 32, 16, 8) if n % t == 0)
    return pl.pallas_call(
        functools.partial(_route_kernel, tm=tm),
        out_shape=(jax.ShapeDtypeStruct((n, LANES), I32), jax.ShapeDtypeStruct((8, LANES), F32)),
        grid=(2, n // tr),
        in_specs=[pl.BlockSpec((tr, LANES), lambda ph, i: (i, 0))],
        out_specs=(pl.BlockSpec((tr, LANES), lambda ph, i: (i * ph, 0)), pl.BlockSpec((8, LANES), lambda ph, i: (0, 0))),
        scratch_shapes=[pltpu.VMEM((1, LANES), F32), pltpu.VMEM((1, LANES), F32)],
        compiler_params=_cparams("arbitrary", "arbitrary"),
        name="moe_route",
    )(top_i)


def _row_copy(src, src_row, dst, dst_row, sem):
    return pltpu.make_async_copy(src.at[pl.ds(src_row, 1), :], dst.at[pl.ds(dst_row, 1), :], sem)


def _dispatch_kernel(pad_ref, dest_ref, hp_ref, hs_ref, xs_hbm, zero, sem, *, steps_p):
    i = pl.program_id(0)
    tm = hp_ref.shape[0]

    def for_pad_rows(fn):
        def group(e, carry):
            def row(j, carry2):
                fn(pad_ref[0, e] + j)
                return carry2
            return lax.fori_loop(0, pad_ref[1, e], row, carry)
        lax.fori_loop(0, pad_ref.shape[1], group, 0)

    @pl.when(i == 0)
    def _():
        zero[...] = jnp.zeros(zero.shape, F32)
        for_pad_rows(lambda row: _row_copy(zero, 0, xs_hbm, row, sem.at[1]).start())
        for_pad_rows(lambda row: _row_copy(zero, 0, xs_hbm, 0, sem.at[1]).wait())

    def scatter(h_ref):
        def body(r, carry):
            for k in range(TOP_K):
                _row_copy(h_ref, r, xs_hbm, dest_ref[0, r * TOP_K + k], sem.at[0]).start()
            return carry
        lax.fori_loop(0, tm, body, 0, unroll=4)
        for k in range(TOP_K):
            pltpu.make_async_copy(h_ref, xs_hbm.at[pl.ds(0, tm), :], sem.at[0]).wait()

    @pl.when(i < steps_p)
    def _():
        scatter(hp_ref)

    @pl.when(i >= steps_p)
    def _():
        scatter(hs_ref)


def _dispatch(h2_p, h2_s, dest, pad_rows, n_rows):
    (np_, d), ns = h2_p.shape, h2_s.shape[0]
    tm = next(t for t in (256, 128, 64, 32, 16, 8) if np_ % t == 0 and ns % t == 0)
    steps_p, steps = np_ // tm, (np_ + ns) // tm
    dest3 = dest[:, :TOP_K].reshape(steps, 1, tm * TOP_K)
    return pl.pallas_call(
        functools.partial(_dispatch_kernel, steps_p=steps_p),
        out_shape=jax.ShapeDtypeStruct((n_rows, d), F32),
        grid_spec=pltpu.PrefetchScalarGridSpec(
            num_scalar_prefetch=1, grid=(steps,),
            in_specs=[pl.BlockSpec((None, 1, tm * TOP_K), lambda i, pad: (i, 0, 0), memory_space=pltpu.SMEM),
                      pl.BlockSpec((tm, d), lambda i, pad: (jnp.minimum(i, steps_p - 1), 0)),
                      pl.BlockSpec((tm, d), lambda i, pad: (jnp.maximum(i - steps_p, 0), 0))],
            out_specs=pl.BlockSpec(memory_space=pl.ANY),
            scratch_shapes=[pltpu.VMEM((8, d), F32), pltpu.SemaphoreType.DMA((2,))]),
        compiler_params=_cparams("arbitrary"),
        name="moe_dispatch",
    )(pad_rows, dest3, h2_p, h2_s)


def _new_expert(te_ref, t):
    prev = te_ref[jnp.maximum(t - 1, 0)]
    return jnp.logical_or(t == 0, te_ref[t] != prev)


def _moe_up_kernel(te_ref, nu_ref, xs_ref, w_ref, b_ref, act_ref, wbf):
    t = pl.program_id(1)

    @pl.when(t < nu_ref[0])
    def _():
        @pl.when(_new_expert(te_ref, t))
        def _():
            wbf[...] = w_ref[...].astype(BF16)

        gu = _dot(xs_ref[...].astype(BF16), wbf[...]) + b_ref[...]
        tn = gu.shape[1]
        nxt = pltpu.roll(gu, tn - 1, 1)
        glu = jnp.minimum(gu, SWIGLU_LIMIT)
        lin = jnp.clip(nxt, -SWIGLU_LIMIT, SWIGLU_LIMIT)
        act = glu * _sigmoid(SWIGLU_ALPHA * glu) * (lin + 1.0)
        even = lax.broadcasted_iota(I32, act.shape, 1) % 2 == 0
        act = jnp.where(even, act, 0.0).astype(BF16)
        blk = 2 * LANES
        r = lax.broadcasted_iota(I32, (blk, LANES), 0)
        c = lax.broadcasted_iota(I32, (blk, LANES), 1)
        pick = jnp.where(r == 2 * c, 1.0, 0.0).astype(BF16)
        for j in range(tn // blk):
            act_ref[:, j * LANES:(j + 1) * LANES] = _dot(act[:, j * blk:(j + 1) * blk], pick).astype(act_ref.dtype)

    @pl.when(t >= nu_ref[0])
    def _():
        act_ref[...] = jnp.zeros(act_ref.shape, act_ref.dtype)


def _moe_up(xs, w_up, b_up, tile_expert, n_used, tm, n_tiles):
    n_experts, d, two_f = w_up.shape
    tn = min(MOE_UP_COLS, two_f)

    def tile(t, nu):
        return jnp.minimum(t, nu[0] - 1)
    return pl.pallas_call(
        _moe_up_kernel,
        out_shape=jax.ShapeDtypeStruct((n_tiles * tm, two_f // 2), BF16),
        grid_spec=pltpu.PrefetchScalarGridSpec(
            num_scalar_prefetch=2, grid=(two_f // tn, n_tiles),
            in_specs=[pl.BlockSpec((tm, d), lambda f, t, te, nu: (tile(t, nu), 0)),
                      pl.BlockSpec((None, d, tn), lambda f, t, te, nu: (te[tile(t, nu)], 0, f)),
                      pl.BlockSpec((None, 1, tn), lambda f, t, te, nu: (te[tile(t, nu)], 0, f))],
            out_specs=pl.BlockSpec((tm, tn // 2), lambda f, t, te, nu: (t, f)),
            scratch_shapes=[pltpu.VMEM((d, tn), BF16)]),
        compiler_params=_cparams("arbitrary", "arbitrary"),
        name="moe_up",
    )(tile_expert, n_used, xs, w_up, b_up.reshape(n_experts, 1, two_f))


def _moe_down_kernel(te_ref, nu_ref, act_ref, w_ref, b_ref, y_ref, wbf):
    t = pl.program_id(1)

    @pl.when(t < nu_ref[0])
    def _():
        @pl.when(_new_expert(te_ref, t))
        def _():
            wbf[...] = w_ref[...].astype(BF16)

        y_ref[...] = _dot(act_ref[...], wbf[...]) + b_ref[...]

    @pl.when(t >= nu_ref[0])
    def _():
        y_ref[...] = jnp.zeros(y_ref.shape, y_ref.dtype)


def _moe_down(act, w_down, b_down, tile_expert, n_used, tm, n_tiles):
    n_experts, f, d = w_down.shape
    tn = min(MOE_DOWN_COLS, d)

    def tile(t, nu):
        return jnp.minimum(t, nu[0] - 1)
    return pl.pallas_call(
        _moe_down_kernel,
        out_shape=jax.ShapeDtypeStruct((n_tiles * tm, d), F32),
        grid_spec=pltpu.PrefetchScalarGridSpec(
            num_scalar_prefetch=2, grid=(d // tn, n_tiles),
            in_specs=[pl.BlockSpec((tm, f), lambda j, t, te, nu: (tile(t, nu), 0)),
                      pl.BlockSpec((None, f, tn), lambda j, t, te, nu: (te[tile(t, nu)], 0, j)),
                      pl.BlockSpec((None, 1, tn), lambda j, t, te, nu: (te[tile(t, nu)], 0, j))],
            out_specs=pl.BlockSpec((tm, tn), lambda j, t, te, nu: (t, j)),
            scratch_shapes=[pltpu.VMEM((f, tn), BF16)]),
        compiler_params=_cparams("arbitrary", "arbitrary"),
        name="moe_down",
    )(tile_expert, n_used, act, w_down, b_down.reshape(n_experts, 1, d))


def _combine_kernel(dest_ref, dest_next_ref, ys_hbm, x1_ref, gate_ref, g2_ref, npost_ref, y_ref, buf, sem):
    i = pl.program_id(0)
    n_steps = pl.num_programs(0)
    tm = x1_ref.shape[0]

    def issue(idx_ref, slot):
        def body(r, carry):
            for k in range(TOP_K):
                _row_copy(ys_hbm, idx_ref[0, r * TOP_K + k], buf.at[slot, k], r, sem.at[slot]).start()
            return carry
        lax.fori_loop(0, tm, body, 0, unroll=4)

    @pl.when(i == 0)
    def _():
        issue(dest_ref, 0)

    @pl.when(i + 1 < n_steps)
    def _():
        issue(dest_next_ref, (i + 1) % 2)

    slot = i % 2
    for k in range(TOP_K):
        pltpu.make_async_copy(ys_hbm.at[pl.ds(0, tm), :], buf.at[slot, k], sem.at[slot]).wait()
    gate = gate_ref[...]
    moe = gate[:, 0:1] * buf[slot, 0]
    for k in range(1, TOP_K):
        moe = moe + gate[:, k:k + 1] * buf[slot, k]
    y_ref[...] = x1_ref[...] + g2_ref[...] * (_rms(moe) * npost_ref[...])


def _moe_combine(ys, dest, gates, x1, mod, npost):
    n, d = x1.shape
    tm = min(COMBINE_TILE, n)
    steps = n // tm
    dest3 = dest[:, :TOP_K].reshape(steps, 1, tm * TOP_K)
    row = pl.BlockSpec((tm, d), lambda i: (i, 0))
    return pl.pallas_call(
        _combine_kernel,
        out_shape=jax.ShapeDtypeStruct((n, d), F32),
        grid=(steps,),
        in_specs=[pl.BlockSpec((None, 1, tm * TOP_K), lambda i: (i, 0, 0), memory_space=pltpu.SMEM),
                  pl.BlockSpec((None, 1, tm * TOP_K), lambda i: (jnp.minimum(i + 1, steps - 1), 0, 0),
                               memory_space=pltpu.SMEM),
                  pl.BlockSpec(memory_space=pl.ANY), row, pl.BlockSpec((tm, LANES), lambda i: (i, 0)),
                  mod.spec(5, tm), pl.BlockSpec((1, d), lambda i: (0, 0))],
        out_specs=row,
        scratch_shapes=[pltpu.VMEM((2, TOP_K, tm, d), F32), pltpu.SemaphoreType.DMA((2,))],
        compiler_params=_cparams("arbitrary"),
        name="moe_combine",
    )(dest3, dest3, ys, x1, gates, mod.array, npost)


def _layer(l, x_prompt, x_sample, c_prompt, c_sample, cache_k_fox, cache_v_fox, cache_logf_fox,
           cache_k_sb, cache_v_sb, page_table, w_ada, b_ada, norm_pre_mix, norm_post_mix,
           norm_pre_ffn, norm_post_ffn, w_in, b_forget, w_out, w_router, b_router,
           w_up, b_up, w_down, b_down):
    batch, seq, d = x_prompt.shape
    bs, n_new, _ = x_sample.shape
    n_heads = d // HEAD_DIM
    n_kv = n_heads // GROUP
    kv_w = n_kv * HEAD_DIM
    n_experts = w_router.shape[-1]
    pool, page = cache_k_fox.shape[1], cache_k_fox.shape[2]
    np_, ns = batch * seq, bs * n_new

    f0 = d + 2 * kv_w
    w_l = w_in[l]
    w_main = jnp.concatenate([w_l[:, :f0], w_l[:, f0 + n_heads:]], axis=1).astype(BF16)
    wf_pad = jnp.pad(w_l[:, f0:f0 + n_heads], ((0, 0), (0, LANES - n_heads))).astype(BF16)
    bf_pad = jnp.pad(b_forget[l], (0, LANES - n_heads)).reshape(1, LANES)
    wo_bf = w_out[l].astype(BF16)
    wr_pad = jnp.pad(w_router[l], ((0, 0), (0, LANES - n_experts))).astype(BF16)
    br_pad = jnp.pad(b_router[l], (0, LANES - n_experts), constant_values=NEG_INF).reshape(1, LANES)
    norms = [n_[l].reshape(1, d) for n_ in (norm_pre_mix, norm_post_mix, norm_pre_ffn, norm_post_ffn)]

    n_c = batch + bs
    c_all = jnp.concatenate([c_prompt, c_sample, jnp.zeros((-n_c % 8, d), F32)], axis=0)
    mod_all = _modulation(c_all, w_ada[l], b_ada[l])
    mod_p = _prompt_mod(mod_all[:batch], seq, d)
    mod_s = _sample_mod(mod_all[batch:n_c], n_new, d)

    xp = x_prompt.reshape(np_, d)
    xs = x_sample.reshape(ns, d)
    scale = HEAD_DIM ** -0.5
    col = {"qa": 0, "ka": d, "va": d + kv_w, "qb": f0, "kb": f0 + d, "vb": f0 + d + kv_w, "gate": f0 + d + 2 * kv_w}

    def mixer_inputs(x, mod):
        h, logf = _prenorm(x, mod, norms[0], wf_pad, bf_pad, n_heads)
        qa, = _project(h, w_main, col["qa"], d, (BF16,), scale)
        qb, = _project(h, w_main, col["qb"], d, (BF16,), scale)
        ka = _project(h, w_main, col["ka"], kv_w, (F32, BF16))
        va = _project(h, w_main, col["va"], kv_w, (F32, BF16))
        kb = _project(h, w_main, col["kb"], kv_w, (F32, BF16))
        vb = _project(h, w_main, col["vb"], kv_w, (F32, BF16))
        gates, = _project(h, w_main, col["gate"], 2 * d, (F32,))
        return logf, qa, qb, ka, va, kb, vb, gates

    logf_p, qa, qb, ka, va, kb, vb, gates = mixer_inputs(xp, mod_p)
    f_rows = _cumsum_rows(logf_p.reshape(batch, seq, n_heads).transpose(0, 2, 1))
    neg_f = -f_rows.transpose(0, 2, 1).reshape(batch, seq, n_kv, GROUP)
    pieces = []
    for _ in range(2):
        top = lax.bitcast_convert_type(
            lax.bitcast_convert_type(neg_f, jnp.uint32) & jnp.uint32(0xFFFF0000), F32)
        pieces.append(top.astype(BF16))
        neg_f = neg_f - top
    pieces.append(neg_f.astype(BF16))
    f_cols = jnp.stack(pieces, axis=-1).reshape(batch, seq, n_kv, 3 * GROUP)
    ka_aug = jnp.concatenate(
        [ka[1].reshape(batch, seq, n_kv, HEAD_DIM), f_cols,
         jnp.zeros((batch, seq, n_kv, HEAD_DIM - 3 * GROUP), BF16)], axis=-1).reshape(np_, 2 * kv_w)
    to_rows = lambda v: v.reshape(batch, seq, kv_w).transpose(0, 2, 1)
    o_p = _prompt_attention(qa, qb, ka_aug, to_rows(va[1]), kb[1], to_rows(vb[1]), gates, batch, seq, n_kv)
    state_p = (ka[0].reshape(batch, seq, n_kv, HEAD_DIM), va[0].reshape(batch, seq, n_kv, HEAD_DIM),
               logf_p.reshape(batch, seq, n_heads),
               kb[0].reshape(batch, seq, n_kv, HEAD_DIM), vb[0].reshape(batch, seq, n_kv, HEAD_DIM))

    logf_s, qa, qb, ka, va, kb, vb, gates = mixer_inputs(xs, mod_s)
    rows = n_new * n_heads
    pad_new = lambda a: jnp.pad(a.reshape(bs, n_new, kv_w), ((0, 0), (0, -n_new % 8), (0, 0)))
    new_tok = LANES // n_kv
    flat_new = lambda a: jnp.pad(a.reshape(bs, n_new, n_kv, HEAD_DIM),
                                 ((0, 0), (0, new_tok - n_new), (0, 0), (0, 0))).reshape(bs, LANES, HEAD_DIM)
    lf_s = logf_s.reshape(bs, n_new, n_heads)
    own = (jnp.arange(LANES, dtype=I32)[None, :] % n_kv) == (jnp.arange(n_heads, dtype=I32)[:, None] // GROUP)
    bias_new = jnp.pad(jnp.repeat(-jnp.cumsum(lf_s, axis=1).transpose(0, 2, 1), n_kv, axis=2),
                       ((0, 0), (0, 0), (0, LANES - n_new * n_kv)))
    bias_new = jnp.where(own[None], bias_new, NEG_INF)
    o_s = _sample_attention(
        qa.reshape(bs, rows, HEAD_DIM), qb.reshape(bs, rows, HEAD_DIM),
        flat_new(ka[0]), flat_new(va[0]), pad_new(kb[0]), pad_new(vb[0]), bias_new,
        gates.reshape(bs, 2 * rows, HEAD_DIM), cache_logf_fox[l].transpose(0, 2, 1),
        *(c[l].reshape(pool, page * n_kv, HEAD_DIM) for c in (cache_k_fox, cache_v_fox, cache_k_sb, cache_v_sb)),
        page_table, n_heads, n_kv, n_new).reshape(ns, d)
    state_s = (ka[0].reshape(bs, n_new, n_kv, HEAD_DIM), va[0].reshape(bs, n_new, n_kv, HEAD_DIM), lf_s,
               kb[0].reshape(bs, n_new, n_kv, HEAD_DIM), vb[0].reshape(bs, n_new, n_kv, HEAD_DIM))

    x1_p, h2_p, ti_p, tg_p = _post_attention(o_p, wo_bf, xp, mod_p, norms[1], norms[2], wr_pad, br_pad)
    x1_s, h2_s, ti_s, tg_s = _post_attention(o_s, wo_bf, xs, mod_s, norms[1], norms[2], wr_pad, br_pad)

    n_tok = np_ + ns
    tm = MOE_TILE
    n_tiles = (n_tok * TOP_K) // tm + n_experts
    dest, counts = _route(jnp.concatenate([ti_p, ti_s], axis=0), tm)
    counts = counts[0, :n_experts].astype(I32)
    tiles_e = (counts + tm - 1) // tm
    tile_end = jnp.cumsum(tiles_e)
    n_used = tile_end[-1:]
    tile_expert = jnp.minimum(jnp.sum(tile_end[None, :] <= jnp.arange(n_tiles, dtype=I32)[:, None], axis=1),
                              n_experts - 1).astype(I32)
    group_start = (tile_end - tiles_e) * tm
    pad_rows = jnp.stack([jnp.append(group_start + counts, n_used[0] * tm),
                          jnp.append(tiles_e * tm - counts, (n_tiles - n_used[0]) * tm)]).astype(I32)
    xsort = _dispatch(h2_p, h2_s, dest, pad_rows, n_tiles * tm)
    act = _moe_up(xsort, w_up[l], b_up[l], tile_expert, n_used, tm, n_tiles)
    ys = _moe_down(act, w_down[l], b_down[l], tile_expert, n_used, tm, n_tiles)
    y_p = _moe_combine(ys, dest[:np_], tg_p, x1_p, mod_p, norms[3])
    y_s = _moe_combine(ys, dest[np_:], tg_s, x1_s, mod_s, norms[3])
    return y_p.reshape(batch, seq, d), y_s.reshape(bs, n_new, d), state_p, state_s


def kernel(x_prompt, x_sample, c_prompt, c_sample, cache_k_fox, cache_v_fox, cache_logf_fox, cache_k_sb, cache_v_sb, page_table, w_ada, b_ada, norm_pre_mix, norm_post_mix, norm_pre_ffn, norm_post_ffn, w_in, b_forget, w_out, w_router, b_router, w_up, b_up, w_down, b_down):
    depth = w_in.shape[0]
    xp, xs = x_prompt, x_sample
    p_states, s_states = [], []
    for l in range(depth):
        xp, xs, sp, ss = _layer(l, xp, xs, c_prompt, c_sample, cache_k_fox, cache_v_fox, cache_logf_fox,
                                cache_k_sb, cache_v_sb, page_table, w_ada, b_ada, norm_pre_mix, norm_post_mix,
                                norm_pre_ffn, norm_post_ffn, w_in, b_forget, w_out, w_router, b_router,
                                w_up, b_up, w_down, b_down)
        p_states.append(sp)
        s_states.append(ss)
    k_fox_p, v_fox_p, logf_fox_p, k_sb_p, v_sb_p = (jnp.stack(t) for t in zip(*p_states))
    k_fox_s, v_fox_s, logf_fox_s, k_sb_s, v_sb_s = (jnp.stack(t) for t in zip(*s_states))
    return (xp, xs, k_fox_p, v_fox_p, logf_fox_p, k_sb_p, v_sb_p,
            k_fox_s, v_fox_s, logf_fox_s, k_sb_s, v_sb_s)
```

```python
import functools

import jax
import jax.numpy as jnp
from jax import lax
from jax.experimental import pallas as pl
from jax.experimental.pallas import tpu as pltpu

F32 = jnp.float32
BF16 = jnp.bfloat16
I32 = jnp.int32

HEAD_DIM = 128
GROUP = 2
TOP_K = 4
SWIGLU_LIMIT = 7.0
SWIGLU_ALPHA = 1.702
RMS_EPS = 1e-6
NEG_INF = -1e30
EXP_UNDERFLOW = -104.0
LANES = 128
VMEM_LIMIT_BYTES = 48 * 1024 * 1024

ROW_TILE = 512
PROJ_COLS = 1024
ATTN_TILE = 256
ATTN_KEY_BLOCKS = 2
FOX_RING = 8
SB_AHEAD = 2
POST_TILE = 256
MOE_TILE = 256
MOE_UP_COLS = 1024
MOE_DOWN_COLS = 1024
COMBINE_TILE = 128


def _cparams(*sem):
    return pltpu.CompilerParams(dimension_semantics=sem, vmem_limit_bytes=VMEM_LIMIT_BYTES)


def _sigmoid(x):
    return 1.0 / (1.0 + jnp.exp(-x))


def _log_sigmoid(x):
    return jnp.minimum(x, 0.0) - jnp.log1p(jnp.exp(-jnp.abs(x)))


def _rms(x):
    return x * lax.rsqrt(jnp.mean(x * x, axis=-1, keepdims=True) + RMS_EPS)


def _split_bf16(x, parts):
    out = []
    for _ in range(parts - 1):
        p = x.astype(BF16)
        out.append(p)
        x = x - p.astype(F32)
    out.append(x.astype(BF16))
    return out


def _dot_nt(a, b):
    return lax.dot_general(a, b, (((1,), (1,)), ((), ())), preferred_element_type=F32)


def _dot(a, b):
    return jnp.dot(a, b, preferred_element_type=F32)


def _strict_lower_ones(n):
    r = lax.broadcasted_iota(I32, (n, n), 0)
    c = lax.broadcasted_iota(I32, (n, n), 1)
    return jnp.where(r > c, 1.0, 0.0).astype(BF16)


def _ada_kernel(c_ref, w_ref, b_ref, o_ref):
    c = c_ref[...]
    s = (c * _sigmoid(c)).astype(BF16)
    o_ref[...] = _dot(s, w_ref[...].astype(BF16)) + b_ref[...]


def _modulation(c_all, w_ada, b_ada):
    m, d = c_all.shape
    n6 = w_ada.shape[1]
    tn = PROJ_COLS
    return pl.pallas_call(
        _ada_kernel,
        out_shape=jax.ShapeDtypeStruct((m, n6), F32),
        grid=(n6 // tn,),
        in_specs=[pl.BlockSpec((m, d), lambda j: (0, 0)),
                  pl.BlockSpec((d, tn), lambda j: (0, j)),
                  pl.BlockSpec((1, tn), lambda j: (0, j))],
        out_specs=pl.BlockSpec((m, tn), lambda j: (0, j)),
        compiler_params=_cparams("arbitrary"),
        name="ada",
    )(c_all, w_ada, b_ada.reshape(1, n6))


class _Mod:
    def __init__(self, array, spec_fn):
        self.array = array
        self.spec = spec_fn


def _prompt_mod(mod_p, seq, d):
    arr = mod_p.reshape(mod_p.shape[0], 1, 6 * d)

    def spec(k, tm):
        per_seq = seq // tm
        return pl.BlockSpec((None, 1, d), lambda i, *_: (i // per_seq, 0, k))
    return _Mod(arr, spec)


def _sample_mod(mod_s, t, d):
    arr = jnp.repeat(mod_s, t, axis=0)

    def spec(k, tm):
        return pl.BlockSpec((tm, d), lambda i, *_: (i, k))
    return _Mod(arr, spec)


def _prenorm_kernel(x_ref, g_ref, sh_ref, sc_ref, wf_ref, bf_ref, h_ref, lf_ref):
    h = _rms(x_ref[...]) * g_ref[...] * (1.0 + sc_ref[...]) + sh_ref[...]
    hb = h.astype(BF16)
    h_ref[...] = hb
    fa = _dot(hb, wf_ref[...]) + bf_ref[...]
    lf_ref[...] = _log_sigmoid(fa)[:, :lf_ref.shape[1]]


def _prenorm(x, mod, g, wf_pad, bf_pad, n_heads):
    n, d = x.shape
    tm = min(ROW_TILE, n)
    return pl.pallas_call(
        _prenorm_kernel,
        out_shape=(jax.ShapeDtypeStruct((n, d), BF16), jax.ShapeDtypeStruct((n, n_heads), F32)),
        grid=(n // tm,),
        in_specs=[pl.BlockSpec((tm, d), lambda i: (i, 0)),
                  pl.BlockSpec((1, d), lambda i: (0, 0)),
                  mod.spec(0, tm), mod.spec(1, tm),
                  pl.BlockSpec((d, LANES), lambda i: (0, 0)),
                  pl.BlockSpec((1, LANES), lambda i: (0, 0))],
        out_specs=(pl.BlockSpec((tm, d), lambda i: (i, 0)),
                   pl.BlockSpec((tm, n_heads), lambda i: (i, 0))),
        compiler_params=_cparams("arbitrary"),
        name="prenorm",
    )(x, g, mod.array, mod.array, wf_pad, bf_pad)


def _proj_kernel(h_ref, w_ref, *o_refs, scale):
    acc = _dot(h_ref[...], w_ref[...])
    if scale is not None:
        acc = acc * scale
    for o_ref in o_refs:
        o_ref[...] = acc.astype(o_ref.dtype)


def _project(h, w_bf, col0, width, out_dtypes, scale=None):
    n, d = h.shape
    tm = min(ROW_TILE, n)
    tn = min(PROJ_COLS, width)
    cb0 = col0 // tn
    outs = pl.pallas_call(
        functools.partial(_proj_kernel, scale=scale),
        out_shape=tuple(jax.ShapeDtypeStruct((n, width), dt) for dt in out_dtypes),
        grid=(width // tn, n // tm),
        in_specs=[pl.BlockSpec((tm, d), lambda j, i: (i, 0)),
                  pl.BlockSpec((d, tn), lambda j, i: (0, cb0 + j))],
        out_specs=tuple(pl.BlockSpec((tm, tn), lambda j, i: (i, j)) for _ in out_dtypes),
        compiler_params=_cparams("arbitrary", "arbitrary"),
        name="proj",
    )(h, w_bf)
    return outs


def _cumsum_kernel(lf_ref, o_ref):
    nh, s = lf_ref.shape
    blk = LANES
    r = lax.broadcasted_iota(I32, (blk, blk), 0)
    c = lax.broadcasted_iota(I32, (blk, blk), 1)
    upper = jnp.where(r <= c, 1.0, 0.0).astype(BF16)

    def body(i, carry):
        off = pl.multiple_of(i * blk, blk)
        x = lf_ref[:, pl.ds(off, blk)]
        cs = carry
        for p in _split_bf16(x, 3):
            cs = cs + _dot(p, upper)
        o_ref[:, pl.ds(off, blk)] = cs
        return cs[:, blk - 1:blk]

    lax.fori_loop(0, s // blk, body, jnp.zeros((nh, 1), F32))


def _cumsum_rows(lf_t):
    b, nh, s = lf_t.shape
    return pl.pallas_call(
        _cumsum_kernel,
        out_shape=jax.ShapeDtypeStruct((b, nh, s), F32),
        grid=(b,),
        in_specs=[pl.BlockSpec((None, nh, s), lambda i: (i, 0, 0))],
        out_specs=pl.BlockSpec((None, nh, s), lambda i: (i, 0, 0)),
        compiler_params=_cparams("arbitrary"),
        name="cumsum",
    )(lf_t)


def _prompt_attn_kernel(qa_ref, qb_ref, ka_ref, vat_ref, kb_ref, vbt_ref, ga_ref, gb_ref, o_ref, *, tk):
    tq = qa_ref.shape[0]
    hd = HEAD_DIM
    qi = pl.program_id(2)
    q0 = qi * tq
    w = GROUP * tq
    lane = lax.broadcasted_iota(I32, (tq, hd), 1)
    picks = [jnp.where((lane >= 3 * r) & (lane < 3 * r + 3), 1.0, 0.0).astype(BF16) for r in range(GROUP)]
    qa = jnp.concatenate([jnp.concatenate([qa_ref[:, r * hd:(r + 1) * hd], picks[r]], axis=1)
                          for r in range(GROUP)], axis=0)
    qb = jnp.concatenate([qb_ref[:, r * hd:(r + 1) * hd] for r in range(GROUP)], axis=0)

    def key_before_query(k0, n_keys, strict):
        key_pos = k0 + lax.broadcasted_iota(I32, (n_keys, w), 0)
        query_pos = q0 + lax.broadcasted_iota(I32, (n_keys, w), 1) % tq
        return key_pos < query_pos if strict else key_pos <= query_pos

    def fox_block(k0, n_keys, carry, diag):
        m, l, acc = carry
        s = _dot_nt(ka_ref[pl.ds(k0, n_keys), :], qa)
        if diag:
            s = jnp.where(key_before_query(k0, n_keys, False), s, NEG_INF)
        m_new = jnp.maximum(m, jnp.max(s, axis=0, keepdims=True))
        alpha = jnp.exp(m - m_new)
        p = jnp.exp(s - m_new)
        l = alpha * l + jnp.sum(p, axis=0, keepdims=True)
        acc = alpha * acc + _dot(vat_ref[:, pl.ds(k0, n_keys)], p.astype(BF16))
        return m_new, l, acc

    per_big = tk // tq
    carry = (jnp.full((1, w), NEG_INF, F32), jnp.zeros((1, w), F32), jnp.zeros((hd, w), F32))
    carry = lax.fori_loop(
        0, qi // per_big, lambda j, cr: fox_block(pl.multiple_of(j * tk, tk), tk, cr, False), carry)
    rest0 = (qi // per_big) * tk
    carry = lax.fori_loop(
        0, qi % per_big, lambda j, cr: fox_block(pl.multiple_of(rest0 + j * tq, tq), tq, cr, False), carry)
    _, l, acc = fox_block(pl.multiple_of(q0, tq), tq, carry, True)
    oa = acc / l

    r_ = lax.broadcasted_iota(I32, (tq, tq), 0)
    c_ = lax.broadcasted_iota(I32, (tq, tq), 1)
    later_keys = jnp.where(c_ > r_, 1.0, 0.0).astype(BF16)

    def sb_block(kj, c, acc, diag):
        k0 = pl.multiple_of(kj * tq, tq)
        z = _dot_nt(kb_ref[pl.ds(k0, tq), :], qb)
        t = jnp.log1p(jnp.exp(-jnp.abs(z)))
        lneg = -jnp.maximum(z, 0.0) - t
        lpos = jnp.minimum(z, 0.0) - t
        if diag:
            visible = key_before_query(k0, tq, True)
            lneg = jnp.where(visible, lneg, 0.0)
        suf = c
        for piece in _split_bf16(lneg, 2):
            suf = suf + _dot(later_keys, piece)
        wgt = jnp.exp(lpos + suf)
        if diag:
            wgt = jnp.where(visible, wgt, 0.0)
        acc = acc + _dot(vbt_ref[:, pl.ds(k0, tq)], wgt.astype(BF16))
        c = c + jnp.sum(lneg, axis=0, keepdims=True)
        return c, acc

    c, acc = sb_block(qi, jnp.zeros((1, w), F32), jnp.zeros((hd, w), F32), True)

    def cond(st):
        return jnp.logical_and(st[0] >= 0, jnp.max(st[1]) > EXP_UNDERFLOW)

    def body(st):
        c2, acc2 = sb_block(st[0], st[1], st[2], False)
        return st[0] - 1, c2, acc2

    ob = lax.while_loop(cond, body, (qi - 1, c, acc))[2]

    for r in range(GROUP):
        ga = ga_ref[:, r * hd:(r + 1) * hd]
        gb = gb_ref[:, r * hd:(r + 1) * hd]
        oa_r = oa[:, r * tq:(r + 1) * tq].T
        ob_r = ob[:, r * tq:(r + 1) * tq].T
        o_ref[:, r * hd:(r + 1) * hd] = (_sigmoid(ga) * oa_r + _sigmoid(gb) * ob_r).astype(o_ref.dtype)


def _prompt_attention(qa, qb, ka_aug, va_t, kb, vb_t, gates, batch, seq, n_kv):
    n, dq = qa.shape
    tq = min(ATTN_TILE, seq)
    tk = ATTN_KEY_BLOCKS * tq
    nq = seq // tq
    gw = GROUP * HEAD_DIM
    qspec = pl.BlockSpec((tq, gw), lambda b, g, i: (b * nq + i, g))
    vspec = pl.BlockSpec((None, HEAD_DIM, seq), lambda b, g, i: (b, g, 0))
    return pl.pallas_call(
        functools.partial(_prompt_attn_kernel, tk=tk),
        out_shape=jax.ShapeDtypeStruct((n, dq), BF16),
        grid=(batch, n_kv, nq),
        in_specs=[qspec, qspec,
                  pl.BlockSpec((seq, 2 * HEAD_DIM), lambda b, g, i: (b, g)), vspec,
                  pl.BlockSpec((seq, HEAD_DIM), lambda b, g, i: (b, g)), vspec,
                  pl.BlockSpec((tq, gw), lambda b, g, i: (b * nq + i, g)),
                  pl.BlockSpec((tq, gw), lambda b, g, i: (b * nq + i, n_kv + g))],
        out_specs=qspec,
        compiler_params=_cparams("arbitrary", "arbitrary", "arbitrary"),
        name="prompt_attn",
    )(qa, qb, ka_aug, va_t, kb, vb_t, gates, gates)


def _block_diag_queries(q, n_heads, n_kv):
    rows = q.shape[0]
    kv_of_row = (lax.broadcasted_iota(I32, (rows, 1), 0) % n_heads) // GROUP
    blocks = [jnp.where(kv_of_row == g, q, jnp.zeros_like(q)) for g in range(n_kv)]
    return jnp.concatenate(blocks, axis=1), kv_of_row


def _own_head_block(pv, kv_of_row, n_kv):
    out = jnp.zeros((pv.shape[0], HEAD_DIM), F32)
    for g in range(n_kv):
        out = out + jnp.where(kv_of_row == g, pv[:, g * HEAD_DIM:(g + 1) * HEAD_DIM], 0.0)
    return out


def _page_copies(pt_ref, b, p, slot, kc_ref, vc_ref, kbuf, vbuf, sem):
    page = pt_ref[b, p]
    return (pltpu.make_async_copy(kc_ref.at[page], kbuf.at[slot], sem.at[0, slot]),
            pltpu.make_async_copy(vc_ref.at[page], vbuf.at[slot], sem.at[1, slot]))


def _load_page(buf, slot, n_kv):
    tokens = buf.shape[1] // n_kv
    heads = [buf[slot, pl.ds(g, tokens, stride=n_kv), :].astype(BF16) for g in range(n_kv)]
    return jnp.concatenate(heads, axis=1)


def _sample_fox_kernel(pt_ref, q_ref, kn_ref, vn_ref, bn_ref, *rest, n_pages, n_heads, n_kv, n_new):
    lf_refs = rest[:n_pages]
    kc_ref, vc_ref, o_ref, kbuf, vbuf, bias_s, sem = rest[n_pages:]
    b = pl.program_id(0)
    cols = kbuf.shape[1]
    page = cols // n_kv
    rows = q_ref.shape[0]
    last = n_pages - 1
    ring = kbuf.shape[0]
    total = pl.num_programs(0) * n_pages

    def copies(n):
        return _page_copies(pt_ref, n // n_pages, last - n % n_pages, n % ring, kc_ref, vc_ref, kbuf, vbuf, sem)

    @pl.when(b == 0)
    def _():
        for n in range(ring - 1):
            for cp in copies(n):
                cp.start()

    x = jnp.concatenate([lf_refs[p][...] for p in range(n_pages)], axis=0)
    key_of_col = lax.broadcasted_iota(I32, (page, cols), 1) // n_kv
    spread = jnp.where(lax.broadcasted_iota(I32, (page, cols), 0) > key_of_col, 1.0, 0.0).astype(BF16)
    within = jnp.zeros((x.shape[0], cols), F32)
    for piece in _split_bf16(x, 3):
        within = within + _dot(piece, spread)
    totals = jnp.sum(x, axis=1, keepdims=True)
    own = (lax.broadcasted_iota(I32, (n_heads, cols), 1) % n_kv) == (lax.broadcasted_iota(I32, (n_heads, cols), 0) // GROUP)
    other_head = jnp.where(own, 0.0, NEG_INF)
    later = jnp.zeros((n_heads, 1), F32)
    for p in range(last, -1, -1):
        bias_s[p] = within[p * n_heads:(p + 1) * n_heads] + later + other_head
        later = later + totals[p * n_heads:(p + 1) * n_heads]

    q = q_ref[...]

    def attend(kp, vp, bias, mask, carry):
        m, l, acc = carry
        s = _dot_nt(q, kp) + bias
        if mask is not None:
            s = jnp.where(mask, s, NEG_INF)
        m_new = jnp.maximum(m, jnp.max(s, axis=1, keepdims=True))
        alpha = jnp.exp(m - m_new)
        p = jnp.exp(s - m_new)
        l = alpha * l + jnp.sum(p, axis=1, keepdims=True)
        acc = alpha * acc + _dot(p.astype(BF16), vp)
        return m_new, l, acc

    init = (jnp.full((rows, 1), NEG_INF, F32), jnp.zeros((rows, 1), F32), jnp.zeros((rows, HEAD_DIM), F32))
    new_cols = kn_ref.shape[0]
    t_of_row = lax.broadcasted_iota(I32, (rows, new_cols), 0) // n_heads
    j_of_col = lax.broadcasted_iota(I32, (rows, new_cols), 1) // n_kv
    carry = attend(kn_ref[...].astype(BF16), vn_ref[...].astype(BF16),
                   jnp.concatenate([bn_ref[...]] * n_new, axis=0), j_of_col <= t_of_row, init)

    def body(i, carry):
        n = b * n_pages + i
        slot = n % ring
        for cp in copies(n):
            cp.wait()

        @pl.when(n + ring - 1 < total)
        def _():
            for cp in copies(n + ring - 1):
                cp.start()

        bias = jnp.concatenate([bias_s[last - i]] * n_new, axis=0)
        return attend(kbuf[slot].astype(BF16), vbuf[slot].astype(BF16), bias, None, carry)

    _, l, acc = lax.fori_loop(0, n_pages, body, carry)
    o_ref[...] = acc / l


def _sample_sb_kernel(pt_ref, q_ref, kn_ref, vn_ref, oa_ref, gate_ref, kc_ref, vc_ref, o_ref,
                      kbuf, vbuf, sem, *, n_pages, n_heads, n_kv, n_new):
    b = pl.program_id(0)
    page = kbuf.shape[1] // n_kv
    rows = q_ref.shape[0]
    last = n_pages - 1
    ahead = (kbuf.shape[0] - 2) // 2

    def ahead_copies(seq, j):
        return _page_copies(pt_ref, seq, last - j, (seq % 2) * ahead + j, kc_ref, vc_ref, kbuf, vbuf, sem)

    def copies(p):
        return _page_copies(pt_ref, b, p, 2 * ahead + p % 2, kc_ref, vc_ref, kbuf, vbuf, sem)

    @pl.when(b == 0)
    def _():
        for j in range(ahead):
            for cp in ahead_copies(0, j):
                cp.start()

    @pl.when(b + 1 < pl.num_programs(0))
    def _():
        for j in range(ahead):
            for cp in ahead_copies(b + 1, j):
                cp.start()

    qbd, kv_of_row = _block_diag_queries(q_ref[...], n_heads, n_kv)
    t_of_row = lax.broadcasted_iota(I32, (rows, page), 0) // n_heads
    lane = lax.broadcasted_iota(I32, (rows, page), 1)
    ones = _strict_lower_ones(page)

    def attend(kp, vp, mask, c, acc):
        z = _dot_nt(qbd, kp)
        t = jnp.log1p(jnp.exp(-jnp.abs(z)))
        lneg = -jnp.maximum(z, 0.0) - t
        lpos = jnp.minimum(z, 0.0) - t
        if mask is not None:
            lneg = jnp.where(mask, lneg, 0.0)
        suf = c
        for piece in _split_bf16(lneg, 2):
            suf = suf + _dot(piece, ones)
        w = jnp.exp(lpos + suf)
        if mask is not None:
            w = jnp.where(mask, w, 0.0)
        acc = acc + _own_head_block(_dot(w.astype(BF16), vp), kv_of_row, n_kv)
        c = c + jnp.sum(lneg, axis=1, keepdims=True)
        return c, acc

    def pad_page(ref):
        new = ref[...]
        return jnp.concatenate([new, jnp.zeros((page - new.shape[0], new.shape[1]), F32)], axis=0).astype(BF16)

    c, acc = attend(pad_page(kn_ref), pad_page(vn_ref), lane < t_of_row,
                    jnp.zeros((rows, 1), F32), jnp.zeros((rows, HEAD_DIM), F32))

    def alive(c):
        return jnp.max(c) > EXP_UNDERFLOW

    for j in range(ahead):
        for cp in ahead_copies(b, j):
            cp.wait()
        slot = (b % 2) * ahead + j
        c, acc = lax.fori_loop(
            0, alive(c).astype(I32),
            lambda _, st: attend(_load_page(kbuf, slot, n_kv), _load_page(vbuf, slot, n_kv), None, st[0], st[1]),
            (c, acc))

    more = jnp.logical_and(alive(c), last - ahead >= 0)

    @pl.when(more)
    def _():
        for cp in copies(last - ahead):
            cp.start()

    def cond(st):
        return jnp.logical_and(st[0] >= 0, alive(st[1]))

    def body(st):
        p = st[0]
        for cp in copies(p):
            cp.wait()

        @pl.when(p > 0)
        def _():
            for cp in copies(p - 1):
                cp.start()

        slot = 2 * ahead + p % 2
        c2, acc2 = attend(_load_page(kbuf, slot, n_kv), _load_page(vbuf, slot, n_kv), None, st[1], st[2])
        return p - 1, c2, acc2

    p_end, _, ob = lax.while_loop(cond, body, (jnp.int32(last - ahead), c, acc))

    @pl.when(jnp.logical_and(more, p_end >= 0))
    def _():
        for cp in copies(jnp.maximum(p_end, 0)):
            cp.wait()

    ga = jnp.concatenate([gate_ref[t * 2 * n_heads:t * 2 * n_heads + n_heads, :] for t in range(n_new)], axis=0)
    gb = jnp.concatenate([gate_ref[t * 2 * n_heads + n_heads:(t + 1) * 2 * n_heads, :] for t in range(n_new)], axis=0)
    o_ref[...] = (_sigmoid(ga) * oa_ref[...] + _sigmoid(gb) * ob).astype(o_ref.dtype)


def _sample_attention(qa, qb, ka_flat, va_flat, kb_new, vb_new, bias_new, gates, lf_pool_t,
                      ck_fox, cv_fox, ck_sb, cv_sb, page_table, n_heads, n_kv, n_new):
    bs, n_pages = page_table.shape
    pool, page_rows, _ = ck_fox.shape
    page = page_rows // n_kv
    w = n_kv * HEAD_DIM
    rows = n_new * n_heads
    seq_rows = pl.BlockSpec((None, rows, HEAD_DIM), lambda b, pt: (b, 0, 0))
    new_flat = pl.BlockSpec((None, ka_flat.shape[1], HEAD_DIM), lambda b, pt: (b, 0, 0))
    new_kv = pl.BlockSpec((None, kb_new.shape[1], w), lambda b, pt: (b, 0, 0))
    any_spec = pl.BlockSpec(memory_space=pl.ANY)

    def page_buffers(n):
        return [pltpu.VMEM((n, page_rows, HEAD_DIM), F32), pltpu.VMEM((n, page_rows, HEAD_DIM), F32)]

    lf_specs = [pl.BlockSpec((None, n_heads, page), functools.partial(lambda b, pt, p: (pt[b, p], 0, 0), p=p))
                for p in range(n_pages)]
    oa = pl.pallas_call(
        functools.partial(_sample_fox_kernel, n_pages=n_pages, n_heads=n_heads, n_kv=n_kv, n_new=n_new),
        out_shape=jax.ShapeDtypeStruct((bs, rows, HEAD_DIM), F32),
        grid_spec=pltpu.PrefetchScalarGridSpec(
            num_scalar_prefetch=1, grid=(bs,),
            in_specs=[seq_rows, new_flat, new_flat,
                      pl.BlockSpec((None, n_heads, bias_new.shape[2]), lambda b, pt: (b, 0, 0))]
                     + lf_specs + [any_spec, any_spec],
            out_specs=seq_rows,
            scratch_shapes=page_buffers(FOX_RING) + [pltpu.VMEM((n_pages, n_heads, page_rows), F32),
                                                     pltpu.SemaphoreType.DMA((2, FOX_RING))]),
        compiler_params=_cparams("arbitrary"),
        name="sample_fox",
    )(page_table, qa, ka_flat, va_flat, bias_new, *([lf_pool_t] * n_pages), ck_fox, cv_fox)

    return pl.pallas_call(
        functools.partial(_sample_sb_kernel, n_pages=n_pages, n_heads=n_heads, n_kv=n_kv, n_new=n_new),
        out_shape=jax.ShapeDtypeStruct((bs, rows, HEAD_DIM), BF16),
        grid_spec=pltpu.PrefetchScalarGridSpec(
            num_scalar_prefetch=1, grid=(bs,),
            in_specs=[seq_rows, new_kv, new_kv, seq_rows,
                      pl.BlockSpec((None, 2 * rows, HEAD_DIM), lambda b, pt: (b, 0, 0)), any_spec, any_spec],
            out_specs=seq_rows,
            scratch_shapes=page_buffers(2 * SB_AHEAD + 2) + [pltpu.SemaphoreType.DMA((2, 2 * SB_AHEAD + 2))]),
        compiler_params=_cparams("arbitrary"),
        name="sample_sb",
    )(page_table, qb, kb_new, vb_new, oa, gates, ck_sb, cv_sb)


def _post_kernel(o_ref, wo_ref, x_ref, g1_ref, sh2_ref, sc2_ref, npm_ref, npf_ref, wr_ref, br_ref,
                 x1_ref, h2_ref, ti_ref, tg_ref):
    m = _dot(o_ref[...], wo_ref[...])
    x1 = x_ref[...] + g1_ref[...] * (_rms(m) * npm_ref[...])
    x1_ref[...] = x1
    h2 = _rms(x1) * npf_ref[...] * (1.0 + sc2_ref[...]) + sh2_ref[...]
    h2_ref[...] = h2
    logits = _dot(h2.astype(BF16), wr_ref[...]) + br_ref[...]
    lane = lax.broadcasted_iota(I32, logits.shape, 1)
    lane_f = lane.astype(F32)
    work = logits
    vals, ids = [], []
    for _ in range(TOP_K):
        mk = jnp.max(work, axis=1, keepdims=True)
        ik = jnp.min(jnp.where(work == mk, lane_f, float(LANES)), axis=1, keepdims=True)
        vals.append(mk)
        ids.append(ik)
        work = jnp.where(lane_f == ik, -jnp.inf, work)
    es = [jnp.exp(v - vals[0]) for v in vals]
    den = es[0]
    for e in es[1:]:
        den = den + e
    ti = jnp.zeros(logits.shape, F32)
    tg = jnp.zeros(logits.shape, F32)
    for k in range(TOP_K):
        ti = jnp.where(lane == k, ids[k], ti)
        tg = jnp.where(lane == k, es[k] / den, tg)
    ti_ref[...] = ti.astype(I32)
    tg_ref[...] = tg


def _post_attention(o, wo_bf, x, mod, npm, npf, wr_pad, br_pad):
    n, d = x.shape
    tm = min(POST_TILE, n)
    row = pl.BlockSpec((tm, d), lambda i: (i, 0))
    vec = pl.BlockSpec((1, d), lambda i: (0, 0))
    small = pl.BlockSpec((tm, LANES), lambda i: (i, 0))
    return pl.pallas_call(
        _post_kernel,
        out_shape=(jax.ShapeDtypeStruct((n, d), F32), jax.ShapeDtypeStruct((n, d), F32),
                   jax.ShapeDtypeStruct((n, LANES), I32), jax.ShapeDtypeStruct((n, LANES), F32)),
        grid=(n // tm,),
        in_specs=[row, pl.BlockSpec((d, d), lambda i: (0, 0)), row,
                  mod.spec(2, tm), mod.spec(3, tm), mod.spec(4, tm), vec, vec,
                  pl.BlockSpec((d, LANES), lambda i: (0, 0)), pl.BlockSpec((1, LANES), lambda i: (0, 0))],
        out_specs=(row, row, small, small),
        compiler_params=_cparams("arbitrary"),
        name="post_attn",
    )(o, wo_bf, x, mod.array, mod.array, mod.array, npm, npf, wr_pad, br_pad)


def _route_kernel(ti_ref, dest_ref, cnt_ref, seen, base, *, tm):
    ph = pl.program_id(0)
    i = pl.program_id(1)
    tr = ti_ref.shape[0]
    lane = lax.broadcasted_iota(I32, (tr, LANES), 1)
    ti = ti_ref[...]
    hits = [lane == ti[:, k:k + 1] for k in range(TOP_K)]
    member = jnp.zeros((tr, LANES), F32)
    for hit in hits:
        member = member + jnp.where(hit, 1.0, 0.0)
    tile_counts = jnp.sum(member, axis=0, keepdims=True)

    @pl.when(jnp.logical_and(ph == 0, i == 0))
    def _():
        seen[...] = jnp.zeros(seen.shape, F32)

    @pl.when(ph == 0)
    def _():
        seen[...] = seen[...] + tile_counts
        dest_ref[...] = jnp.zeros(dest_ref.shape, I32)

    @pl.when(jnp.logical_and(ph == 1, i == 0))
    def _():
        counts = seen[...]
        cnt_ref[...] = jnp.broadcast_to(counts, cnt_ref.shape)
        tiles = jnp.floor((counts + (tm - 1)) * (1.0 / tm))
        r = lax.broadcasted_iota(I32, (LANES, LANES), 0)
        c = lax.broadcasted_iota(I32, (LANES, LANES), 1)
        before = jnp.where(r < c, 1.0, 0.0).astype(BF16)
        first_tile = jnp.zeros((1, LANES), F32)
        for piece in _split_bf16(jnp.broadcast_to(tiles, (8, LANES)), 2):
            first_tile = first_tile + _dot(piece, before)[0:1]
        base[...] = first_tile * tm
        seen[...] = jnp.zeros(seen.shape, F32)

    @pl.when(ph == 1)
    def _():
        r = lax.broadcasted_iota(I32, (tr, tr), 0)
        c = lax.broadcasted_iota(I32, (tr, tr), 1)
        earlier = jnp.where(c < r, 1.0, 0.0).astype(BF16)
        rank = _dot(earlier, member.astype(BF16)) + (base[...] + seen[...])
        dest = jnp.zeros((tr, LANES), F32)
        for k, hit in enumerate(hits):
            dest = jnp.where(lane == k, jnp.sum(jnp.where(hit, rank, 0.0), axis=1, keepdims=True), dest)
        dest_ref[...] = dest.astype(I32)
        seen[...] = seen[...] + tile_counts


def _route(top_i, tm):
    n = top_i.shape[0]
    tr = next(t for t in (256, 128, 64, 32, 16, 8) if n % t == 0)
    return pl.pallas_call(
        functools.partial(_route_kernel, tm=tm),
        out_shape=(jax.ShapeDtypeStruct((n, LANES), I32), jax.ShapeDtypeStruct((8, LANES), F32)),
        grid=(2, n // tr),
        in_specs=[pl.BlockSpec((tr, LANES), lambda ph, i: (i, 0))],
        out_specs=(pl.BlockSpec((tr, LANES), lambda ph, i: (i * ph, 0)), pl.BlockSpec((8, LANES), lambda ph, i: (0, 0))),
        scratch_shapes=[pltpu.VMEM((1, LANES), F32), pltpu.VMEM((1, LANES), F32)],
        compiler_params=_cparams("arbitrary", "arbitrary"),
        name="moe_route",
    )(top_i)


def _row_copy(src, src_row, dst, dst_row, sem):
    return pltpu.make_async_copy(src.at[pl.ds(src_row, 1), :], dst.at[pl.ds(dst_row, 1), :], sem)


def _dispatch_kernel(pad_ref, dest_ref, hp_ref, hs_ref, xs_hbm, zero, sem, *, steps_p):
    i = pl.program_id(0)
    tm = hp_ref.shape[0]

    def for_pad_rows(fn):
        def group(e, carry):
            def row(j, carry2):
                fn(pad_ref[0, e] + j)
                return carry2
            return lax.fori_loop(0, pad_ref[1, e], row, carry)
        lax.fori_loop(0, pad_ref.shape[1], group, 0)

    @pl.when(i == 0)
    def _():
        zero[...] = jnp.zeros(zero.shape, F32)
        for_pad_rows(lambda row: _row_copy(zero, 0, xs_hbm, row, sem.at[1]).start())
        for_pad_rows(lambda row: _row_copy(zero, 0, xs_hbm, 0, sem.at[1]).wait())

    def scatter(h_ref):
        def body(r, carry):
            for k in range(TOP_K):
                _row_copy(h_ref, r, xs_hbm, dest_ref[0, r * TOP_K + k], sem.at[0]).start()
            return carry
        lax.fori_loop(0, tm, body, 0, unroll=4)
        for k in range(TOP_K):
            pltpu.make_async_copy(h_ref, xs_hbm.at[pl.ds(0, tm), :], sem.at[0]).wait()

    @pl.when(i < steps_p)
    def _():
        scatter(hp_ref)

    @pl.when(i >= steps_p)
    def _():
        scatter(hs_ref)


def _dispatch(h2_p, h2_s, dest, pad_rows, n_rows):
    (np_, d), ns = h2_p.shape, h2_s.shape[0]
    tm = next(t for t in (256, 128, 64, 32, 16, 8) if np_ % t == 0 and ns % t == 0)
    steps_p, steps = np_ // tm, (np_ + ns) // tm
    dest3 = dest[:, :TOP_K].reshape(steps, 1, tm * TOP_K)
    return pl.pallas_call(
        functools.partial(_dispatch_kernel, steps_p=steps_p),
        out_shape=jax.ShapeDtypeStruct((n_rows, d), F32),
        grid_spec=pltpu.PrefetchScalarGridSpec(
            num_scalar_prefetch=1, grid=(steps,),
            in_specs=[pl.BlockSpec((None, 1, tm * TOP_K), lambda i, pad: (i, 0, 0), memory_space=pltpu.SMEM),
                      pl.BlockSpec((tm, d), lambda i, pad: (jnp.minimum(i, steps_p - 1), 0)),
                      pl.BlockSpec((tm, d), lambda i, pad: (jnp.maximum(i - steps_p, 0), 0))],
            out_specs=pl.BlockSpec(memory_space=pl.ANY),
            scratch_shapes=[pltpu.VMEM((8, d), F32), pltpu.SemaphoreType.DMA((2,))]),
        compiler_params=_cparams("arbitrary"),
        name="moe_dispatch",
    )(pad_rows, dest3, h2_p, h2_s)


def _new_expert(te_ref, t):
    prev = te_ref[jnp.maximum(t - 1, 0)]
    return jnp.logical_or(t == 0, te_ref[t] != prev)


def _moe_up_kernel(te_ref, nu_ref, xs_ref, w_ref, b_ref, act_ref, wbf):
    t = pl.program_id(1)

    @pl.when(t < nu_ref[0])
    def _():
        @pl.when(_new_expert(te_ref, t))
        def _():
            wbf[...] = w_ref[...].astype(BF16)

        gu = _dot(xs_ref[...].astype(BF16), wbf[...]) + b_ref[...]
        tn = gu.shape[1]
        nxt = pltpu.roll(gu, tn - 1, 1)
        glu = jnp.minimum(gu, SWIGLU_LIMIT)
        lin = jnp.clip(nxt, -SWIGLU_LIMIT, SWIGLU_LIMIT)
        act = glu * _sigmoid(SWIGLU_ALPHA * glu) * (lin + 1.0)
        even = lax.broadcasted_iota(I32, act.shape, 1) % 2 == 0
        act = jnp.where(even, act, 0.0).astype(BF16)
        blk = 2 * LANES
        r = lax.broadcasted_iota(I32, (blk, LANES), 0)
        c = lax.broadcasted_iota(I32, (blk, LANES), 1)
        pick = jnp.where(r == 2 * c, 1.0, 0.0).astype(BF16)
        for j in range(tn // blk):
            act_ref[:, j * LANES:(j + 1) * LANES] = _dot(act[:, j * blk:(j + 1) * blk], pick).astype(act_ref.dtype)

    @pl.when(t >= nu_ref[0])
    def _():
        act_ref[...] = jnp.zeros(act_ref.shape, act_ref.dtype)


def _moe_up(xs, w_up, b_up, tile_expert, n_used, tm, n_tiles):
    n_experts, d, two_f = w_up.shape
    tn = min(MOE_UP_COLS, two_f)

    def tile(t, nu):
        return jnp.minimum(t, nu[0] - 1)
    return pl.pallas_call(
        _moe_up_kernel,
        out_shape=jax.ShapeDtypeStruct((n_tiles * tm, two_f // 2), BF16),
        grid_spec=pltpu.PrefetchScalarGridSpec(
            num_scalar_prefetch=2, grid=(two_f // tn, n_tiles),
            in_specs=[pl.BlockSpec((tm, d), lambda f, t, te, nu: (tile(t, nu), 0)),
                      pl.BlockSpec((None, d, tn), lambda f, t, te, nu: (te[tile(t, nu)], 0, f)),
                      pl.BlockSpec((None, 1, tn), lambda f, t, te, nu: (te[tile(t, nu)], 0, f))],
            out_specs=pl.BlockSpec((tm, tn // 2), lambda f, t, te, nu: (t, f)),
            scratch_shapes=[pltpu.VMEM((d, tn), BF16)]),
        compiler_params=_cparams("arbitrary", "arbitrary"),
        name="moe_up",
    )(tile_expert, n_used, xs, w_up, b_up.reshape(n_experts, 1, two_f))


def _moe_down_kernel(te_ref, nu_ref, act_ref, w_ref, b_ref, y_ref, wbf):
    t = pl.program_id(1)

    @pl.when(t < nu_ref[0])
    def _():
        @pl.when(_new_expert(te_ref, t))
        def _():
            wbf[...] = w_ref[...].astype(BF16)

        y_ref[...] = _dot(act_ref[...], wbf[...]) + b_ref[...]

    @pl.when(t >= nu_ref[0])
    def _():
        y_ref[...] = jnp.zeros(y_ref.shape, y_ref.dtype)


def _moe_down(act, w_down, b_down, tile_expert, n_used, tm, n_tiles):
    n_experts, f, d = w_down.shape
    tn = min(MOE_DOWN_COLS, d)

    def tile(t, nu):
        return jnp.minimum(t, nu[0] - 1)
    return pl.pallas_call(
        _moe_down_kernel,
        out_shape=jax.ShapeDtypeStruct((n_tiles * tm, d), F32),
        grid_spec=pltpu.PrefetchScalarGridSpec(
            num_scalar_prefetch=2, grid=(d // tn, n_tiles),
            in_specs=[pl.BlockSpec((tm, f), lambda j, t, te, nu: (tile(t, nu), 0)),
                      pl.BlockSpec((None, f, tn), lambda j, t, te, nu: (te[tile(t, nu)], 0, j)),
                      pl.BlockSpec((None, 1, tn), lambda j, t, te, nu: (te[tile(t, nu)], 0, j))],
            out_specs=pl.BlockSpec((tm, tn), lambda j, t, te, nu: (t, j)),
            scratch_shapes=[pltpu.VMEM((f, tn), BF16)]),
        compiler_params=_cparams("arbitrary", "arbitrary"),
        name="moe_down",
    )(tile_expert, n_used, act, w_down, b_down.reshape(n_experts, 1, d))


def _combine_kernel(dest_ref, dest_next_ref, ys_hbm, x1_ref, gate_ref, g2_ref, npost_ref, y_ref, buf, sem):
    i = pl.program_id(0)
    n_steps = pl.num_programs(0)
    tm = x1_ref.shape[0]

    def issue(idx_ref, slot):
        def body(r, carry):
            for k in range(TOP_K):
                _row_copy(ys_hbm, idx_ref[0, r * TOP_K + k], buf.at[slot, k], r, sem.at[slot]).start()
            return carry
        lax.fori_loop(0, tm, body, 0, unroll=4)

    @pl.when(i == 0)
    def _():
        issue(dest_ref, 0)

    @pl.when(i + 1 < n_steps)
    def _():
        issue(dest_next_ref, (i + 1) % 2)

    slot = i % 2
    for k in range(TOP_K):
        pltpu.make_async_copy(ys_hbm.at[pl.ds(0, tm), :], buf.at[slot, k], sem.at[slot]).wait()
    gate = gate_ref[...]
    moe = gate[:, 0:1] * buf[slot, 0]
    for k in range(1, TOP_K):
        moe = moe + gate[:, k:k + 1] * buf[slot, k]
    y_ref[...] = x1_ref[...] + g2_ref[...] * (_rms(moe) * npost_ref[...])


def _moe_combine(ys, dest, gates, x1, mod, npost):
    n, d = x1.shape
    tm = min(COMBINE_TILE, n)
    steps = n // tm
    dest3 = dest[:, :TOP_K].reshape(steps, 1, tm * TOP_K)
    row = pl.BlockSpec((tm, d), lambda i: (i, 0))
    return pl.pallas_call(
        _combine_kernel,
        out_shape=jax.ShapeDtypeStruct((n, d), F32),
        grid=(steps,),
        in_specs=[pl.BlockSpec((None, 1, tm * TOP_K), lambda i: (i, 0, 0), memory_space=pltpu.SMEM),
                  pl.BlockSpec((None, 1, tm * TOP_K), lambda i: (jnp.minimum(i + 1, steps - 1), 0, 0),
                               memory_space=pltpu.SMEM),
                  pl.BlockSpec(memory_space=pl.ANY), row, pl.BlockSpec((tm, LANES), lambda i: (i, 0)),
                  mod.spec(5, tm), pl.BlockSpec((1, d), lambda i: (0, 0))],
        out_specs=row,
        scratch_shapes=[pltpu.VMEM((2, TOP_K, tm, d), F32), pltpu.SemaphoreType.DMA((2,))],
        compiler_params=_cparams("arbitrary"),
        name="moe_combine",
    )(dest3, dest3, ys, x1, gates, mod.array, npost)


def _layer(l, x_prompt, x_sample, c_prompt, c_sample, cache_k_fox, cache_v_fox, cache_logf_fox,
           cache_k_sb, cache_v_sb, page_table, w_ada, b_ada, norm_pre_mix, norm_post_mix,
           norm_pre_ffn, norm_post_ffn, w_in, b_forget, w_out, w_router, b_router,
           w_up, b_up, w_down, b_down):
    batch, seq, d = x_prompt.shape
    bs, n_new, _ = x_sample.shape
    n_heads = d // HEAD_DIM
    n_kv = n_heads // GROUP
    kv_w = n_kv * HEAD_DIM
    n_experts = w_router.shape[-1]
    pool, page = cache_k_fox.shape[1], cache_k_fox.shape[2]
    np_, ns = batch * seq, bs * n_new

    f0 = d + 2 * kv_w
    w_l = w_in[l]
    w_main = jnp.concatenate([w_l[:, :f0], w_l[:, f0 + n_heads:]], axis=1).astype(BF16)
    wf_pad = jnp.pad(w_l[:, f0:f0 + n_heads], ((0, 0), (0, LANES - n_heads))).astype(BF16)
    bf_pad = jnp.pad(b_forget[l], (0, LANES - n_heads)).reshape(1, LANES)
    wo_bf = w_out[l].astype(BF16)
    wr_pad = jnp.pad(w_router[l], ((0, 0), (0, LANES - n_experts))).astype(BF16)
    br_pad = jnp.pad(b_router[l], (0, LANES - n_experts), constant_values=NEG_INF).reshape(1, LANES)
    norms = [n_[l].reshape(1, d) for n_ in (norm_pre_mix, norm_post_mix, norm_pre_ffn, norm_post_ffn)]

    n_c = batch + bs
    c_all = jnp.concatenate([c_prompt, c_sample, jnp.zeros((-n_c % 8, d), F32)], axis=0)
    mod_all = _modulation(c_all, w_ada[l], b_ada[l])
    mod_p = _prompt_mod(mod_all[:batch], seq, d)
    mod_s = _sample_mod(mod_all[batch:n_c], n_new, d)

    xp = x_prompt.reshape(np_, d)
    xs = x_sample.reshape(ns, d)
    scale = HEAD_DIM ** -0.5
    col = {"qa": 0, "ka": d, "va": d + kv_w, "qb": f0, "kb": f0 + d, "vb": f0 + d + kv_w, "gate": f0 + d + 2 * kv_w}

    def mixer_inputs(x, mod):
        h, logf = _prenorm(x, mod, norms[0], wf_pad, bf_pad, n_heads)
        qa, = _project(h, w_main, col["qa"], d, (BF16,), scale)
        qb, = _project(h, w_main, col["qb"], d, (BF16,), scale)
        ka = _project(h, w_main, col["ka"], kv_w, (F32, BF16))
        va = _project(h, w_main, col["va"], kv_w, (F32, BF16))
        kb = _project(h, w_main, col["kb"], kv_w, (F32, BF16))
        vb = _project(h, w_main, col["vb"], kv_w, (F32, BF16))
        gates, = _project(h, w_main, col["gate"], 2 * d, (F32,))
        return logf, qa, qb, ka, va, kb, vb, gates

    logf_p, qa, qb, ka, va, kb, vb, gates = mixer_inputs(xp, mod_p)
    f_rows = _cumsum_rows(logf_p.reshape(batch, seq, n_heads).transpose(0, 2, 1))
    neg_f = -f_rows.transpose(0, 2, 1).reshape(batch, seq, n_kv, GROUP)
    pieces = []
    for _ in range(2):
        top = lax.bitcast_convert_type(
            lax.bitcast_convert_type(neg_f, jnp.uint32) & jnp.uint32(0xFFFF0000), F32)
        pieces.append(top.astype(BF16))
        neg_f = neg_f - top
    pieces.append(neg_f.astype(BF16))
    f_cols = jnp.stack(pieces, axis=-1).reshape(batch, seq, n_kv, 3 * GROUP)
    ka_aug = jnp.concatenate(
        [ka[1].reshape(batch, seq, n_kv, HEAD_DIM), f_cols,
         jnp.zeros((batch, seq, n_kv, HEAD_DIM - 3 * GROUP), BF16)], axis=-1).reshape(np_, 2 * kv_w)
    to_rows = lambda v: v.reshape(batch, seq, kv_w).transpose(0, 2, 1)
    o_p = _prompt_attention(qa, qb, ka_aug, to_rows(va[1]), kb[1], to_rows(vb[1]), gates, batch, seq, n_kv)
    state_p = (ka[0].reshape(batch, seq, n_kv, HEAD_DIM), va[0].reshape(batch, seq, n_kv, HEAD_DIM),
               logf_p.reshape(batch, seq, n_heads),
               kb[0].reshape(batch, seq, n_kv, HEAD_DIM), vb[0].reshape(batch, seq, n_kv, HEAD_DIM))

    logf_s, qa, qb, ka, va, kb, vb, gates = mixer_inputs(xs, mod_s)
    rows = n_new * n_heads
    pad_new = lambda a: jnp.pad(a.reshape(bs, n_new, kv_w), ((0, 0), (0, -n_new % 8), (0, 0)))
    new_tok = LANES // n_kv
    flat_new = lambda a: jnp.pad(a.reshape(bs, n_new, n_kv, HEAD_DIM),
                                 ((0, 0), (0, new_tok - n_new), (0, 0), (0, 0))).reshape(bs, LANES, HEAD_DIM)
    lf_s = logf_s.reshape(bs, n_new, n_heads)
    own = (jnp.arange(LANES, dtype=I32)[None, :] % n_kv) == (jnp.arange(n_heads, dtype=I32)[:, None] // GROUP)
    bias_new = jnp.pad(jnp.repeat(-jnp.cumsum(lf_s, axis=1).transpose(0, 2, 1), n_kv, axis=2),
                       ((0, 0), (0, 0), (0, LANES - n_new * n_kv)))
    bias_new = jnp.where(own[None], bias_new, NEG_INF)
    o_s = _sample_attention(
        qa.reshape(bs, rows, HEAD_DIM), qb.reshape(bs, rows, HEAD_DIM),
        flat_new(ka[0]), flat_new(va[0]), pad_new(kb[0]), pad_new(vb[0]), bias_new,
        gates.reshape(bs, 2 * rows, HEAD_DIM), cache_logf_fox[l].transpose(0, 2, 1),
        *(c[l].reshape(pool, page * n_kv, HEAD_DIM) for c in (cache_k_fox, cache_v_fox, cache_k_sb, cache_v_sb)),
        page_table, n_heads, n_kv, n_new).reshape(ns, d)
    state_s = (ka[0].reshape(bs, n_new, n_kv, HEAD_DIM), va[0].reshape(bs, n_new, n_kv, HEAD_DIM), lf_s,
               kb[0].reshape(bs, n_new, n_kv, HEAD_DIM), vb[0].reshape(bs, n_new, n_kv, HEAD_DIM))

    x1_p, h2_p, ti_p, tg_p = _post_attention(o_p, wo_bf, xp, mod_p, norms[1], norms[2], wr_pad, br_pad)
    x1_s, h2_s, ti_s, tg_s = _post_attention(o_s, wo_bf, xs, mod_s, norms[1], norms[2], wr_pad, br_pad)

    n_tok = np_ + ns
    tm = MOE_TILE
    n_tiles = (n_tok * TOP_K) // tm + n_experts
    dest, counts = _route(jnp.concatenate([ti_p, ti_s], axis=0), tm)
    counts = counts[0, :n_experts].astype(I32)
    tiles_e = (counts + tm - 1) // tm
    tile_end = jnp.cumsum(tiles_e)
    n_used = tile_end[-1:]
    tile_expert = jnp.minimum(jnp.sum(tile_end[None, :] <= jnp.arange(n_tiles, dtype=I32)[:, None], axis=1),
                              n_experts - 1).astype(I32)
    group_start = (tile_end - tiles_e) * tm
    pad_rows = jnp.stack([jnp.append(group_start + counts, n_used[0] * tm),
                          jnp.append(tiles_e * tm - counts, (n_tiles - n_used[0]) * tm)]).astype(I32)
    xsort = _dispatch(h2_p, h2_s, dest, pad_rows, n_tiles * tm)
    act = _moe_up(xsort, w_up[l], b_up[l], tile_expert, n_used, tm, n_tiles)
    ys = _moe_down(act, w_down[l], b_down[l], tile_expert, n_used, tm, n_tiles)
    y_p = _moe_combine(ys, dest[:np_], tg_p, x1_p, mod_p, norms[3])
    y_s = _moe_combine(ys, dest[np_:], tg_s, x1_s, mod_s, norms[3])
    return y_p.reshape(batch, seq, d), y_s.reshape(bs, n_new, d), state_p, state_s


def kernel(x_prompt, x_sample, c_prompt, c_sample, cache_k_fox, cache_v_fox, cache_logf_fox, cache_k_sb, cache_v_sb, page_table, w_ada, b_ada, norm_pre_mix, norm_post_mix, norm_pre_ffn, norm_post_ffn, w_in, b_forget, w_out, w_router, b_router, w_up, b_up, w_down, b_down):
    depth = w_in.shape[0]
    xp, xs = x_prompt, x_sample
    p_states, s_states = [], []
    for l in range(depth):
        xp, xs, sp, ss = _layer(l, xp, xs, c_prompt, c_sample, cache_k_fox, cache_v_fox, cache_logf_fox,
                                cache_k_sb, cache_v_sb, page_table, w_ada, b_ada, norm_pre_mix, norm_post_mix,
                                norm_pre_ffn, norm_post_ffn, w_in, b_forget, w_out, w_router, b_router,
                                w_up, b_up, w_down, b_down)
        p_states.append(sp)
        s_states.append(ss)
    k_fox_p, v_fox_p, logf_fox_p, k_sb_p, v_sb_p = (jnp.stack(t) for t in zip(*p_states))
    k_fox_s, v_fox_s, logf_fox_s, k_sb_s, v_sb_s = (jnp.stack(t) for t in zip(*s_states))
    return (xp, xs, k_fox_p, v_fox_p, logf_fox_p, k_sb_p, v_sb_p,
            k_fox_s, v_fox_s, logf_fox_s, k_sb_s, v_sb_s)
```

```python
import functools

import jax
import jax.numpy as jnp
from jax import lax
from jax.experimental import pallas as pl
from jax.experimental.pallas import tpu as pltpu

F32 = jnp.float32
BF16 = jnp.bfloat16
I32 = jnp.int32

HEAD_DIM = 128
GROUP = 2
TOP_K = 4
SWIGLU_LIMIT = 7.0
SWIGLU_ALPHA = 1.702
RMS_EPS = 1e-6
NEG_INF = -1e30
EXP_UNDERFLOW = -104.0
LANES = 128
VMEM_LIMIT_BYTES = 48 * 1024 * 1024

ROW_TILE = 512
PROJ_COLS = 1024
ATTN_TILE = 256
ATTN_KEY_BLOCKS = 2
FOX_RING = 8
SB_AHEAD = 2
POST_TILE = 256
MOE_TILE = 512
MOE_UP_COLS = 1024
MOE_DOWN_COLS = 1024
COMBINE_TILE = 128


def _cparams(*sem):
    return pltpu.CompilerParams(dimension_semantics=sem, vmem_limit_bytes=VMEM_LIMIT_BYTES)


def _sigmoid(x):
    return 1.0 / (1.0 + jnp.exp(-x))


def _log_sigmoid(x):
    return jnp.minimum(x, 0.0) - jnp.log1p(jnp.exp(-jnp.abs(x)))


def _rms(x):
    return x * lax.rsqrt(jnp.mean(x * x, axis=-1, keepdims=True) + RMS_EPS)


def _split_bf16(x, parts):
    out = []
    for _ in range(parts - 1):
        p = x.astype(BF16)
        out.append(p)
        x = x - p.astype(F32)
    out.append(x.astype(BF16))
    return out


def _dot_nt(a, b):
    return lax.dot_general(a, b, (((1,), (1,)), ((), ())), preferred_element_type=F32)


def _dot(a, b):
    return jnp.dot(a, b, preferred_element_type=F32)


def _strict_lower_ones(n):
    r = lax.broadcasted_iota(I32, (n, n), 0)
    c = lax.broadcasted_iota(I32, (n, n), 1)
    return jnp.where(r > c, 1.0, 0.0).astype(BF16)


def _ada_kernel(c_ref, w_ref, b_ref, o_ref):
    c = c_ref[...]
    s = (c * _sigmoid(c)).astype(BF16)
    o_ref[...] = _dot(s, w_ref[...].astype(BF16)) + b_ref[...]


def _modulation(c_all, w_ada, b_ada):
    m, d = c_all.shape
    n6 = w_ada.shape[1]
    tn = PROJ_COLS
    return pl.pallas_call(
        _ada_kernel,
        out_shape=jax.ShapeDtypeStruct((m, n6), F32),
        grid=(n6 // tn,),
        in_specs=[pl.BlockSpec((m, d), lambda j: (0, 0)),
                  pl.BlockSpec((d, tn), lambda j: (0, j)),
                  pl.BlockSpec((1, tn), lambda j: (0, j))],
        out_specs=pl.BlockSpec((m, tn), lambda j: (0, j)),
        compiler_params=_cparams("arbitrary"),
        name="ada",
    )(c_all, w_ada, b_ada.reshape(1, n6))


class _Mod:
    def __init__(self, array, spec_fn):
        self.array = array
        self.spec = spec_fn


def _prompt_mod(mod_p, seq, d):
    arr = mod_p.reshape(mod_p.shape[0], 1, 6 * d)

    def spec(k, tm):
        per_seq = seq // tm
        return pl.BlockSpec((None, 1, d), lambda i, *_: (i // per_seq, 0, k))
    return _Mod(arr, spec)


def _sample_mod(mod_s, t, d):
    arr = jnp.repeat(mod_s, t, axis=0)

    def spec(k, tm):
        return pl.BlockSpec((tm, d), lambda i, *_: (i, k))
    return _Mod(arr, spec)


def _prenorm_kernel(x_ref, g_ref, sh_ref, sc_ref, wf_ref, bf_ref, h_ref, lf_ref):
    h = _rms(x_ref[...]) * g_ref[...] * (1.0 + sc_ref[...]) + sh_ref[...]
    hb = h.astype(BF16)
    h_ref[...] = hb
    fa = _dot(hb, wf_ref[...]) + bf_ref[...]
    lf_ref[...] = _log_sigmoid(fa)[:, :lf_ref.shape[1]]


def _prenorm(x, mod, g, wf_pad, bf_pad, n_heads):
    n, d = x.shape
    tm = min(ROW_TILE, n)
    return pl.pallas_call(
        _prenorm_kernel,
        out_shape=(jax.ShapeDtypeStruct((n, d), BF16), jax.ShapeDtypeStruct((n, n_heads), F32)),
        grid=(n // tm,),
        in_specs=[pl.BlockSpec((tm, d), lambda i: (i, 0)),
                  pl.BlockSpec((1, d), lambda i: (0, 0)),
                  mod.spec(0, tm), mod.spec(1, tm),
                  pl.BlockSpec((d, LANES), lambda i: (0, 0)),
                  pl.BlockSpec((1, LANES), lambda i: (0, 0))],
        out_specs=(pl.BlockSpec((tm, d), lambda i: (i, 0)),
                   pl.BlockSpec((tm, n_heads), lambda i: (i, 0))),
        compiler_params=_cparams("arbitrary"),
        name="prenorm",
    )(x, g, mod.array, mod.array, wf_pad, bf_pad)


def _proj_kernel(h_ref, w_ref, *o_refs, scale):
    acc = _dot(h_ref[...], w_ref[...])
    if scale is not None:
        acc = acc * scale
    for o_ref in o_refs:
        o_ref[...] = acc.astype(o_ref.dtype)


def _project(h, w_bf, col0, width, out_dtypes, scale=None):
    n, d = h.shape
    tm = min(ROW_TILE, n)
    tn = min(PROJ_COLS, width)
    cb0 = col0 // tn
    outs = pl.pallas_call(
        functools.partial(_proj_kernel, scale=scale),
        out_shape=tuple(jax.ShapeDtypeStruct((n, width), dt) for dt in out_dtypes),
        grid=(width // tn, n // tm),
        in_specs=[pl.BlockSpec((tm, d), lambda j, i: (i, 0)),
                  pl.BlockSpec((d, tn), lambda j, i: (0, cb0 + j))],
        out_specs=tuple(pl.BlockSpec((tm, tn), lambda j, i: (i, j)) for _ in out_dtypes),
        compiler_params=_cparams("arbitrary", "arbitrary"),
        name="proj",
    )(h, w_bf)
    return outs


def _cumsum_kernel(lf_ref, o_ref):
    nh, s = lf_ref.shape
    blk = LANES
    r = lax.broadcasted_iota(I32, (blk, blk), 0)
    c = lax.broadcasted_iota(I32, (blk, blk), 1)
    upper = jnp.where(r <= c, 1.0, 0.0).astype(BF16)

    def body(i, carry):
        off = pl.multiple_of(i * blk, blk)
        x = lf_ref[:, pl.ds(off, blk)]
        cs = carry
        for p in _split_bf16(x, 3):
            cs = cs + _dot(p, upper)
        o_ref[:, pl.ds(off, blk)] = cs
        return cs[:, blk - 1:blk]

    lax.fori_loop(0, s // blk, body, jnp.zeros((nh, 1), F32))


def _cumsum_rows(lf_t):
    b, nh, s = lf_t.shape
    return pl.pallas_call(
        _cumsum_kernel,
        out_shape=jax.ShapeDtypeStruct((b, nh, s), F32),
        grid=(b,),
        in_specs=[pl.BlockSpec((None, nh, s), lambda i: (i, 0, 0))],
        out_specs=pl.BlockSpec((None, nh, s), lambda i: (i, 0, 0)),
        compiler_params=_cparams("arbitrary"),
        name="cumsum",
    )(lf_t)


def _prompt_attn_kernel(qa_ref, qb_ref, ka_ref, vat_ref, kb_ref, vbt_ref, ga_ref, gb_ref, o_ref, *, tk):
    tq = qa_ref.shape[0]
    hd = HEAD_DIM
    qi = pl.program_id(2)
    q0 = qi * tq
    w = GROUP * tq
    lane = lax.broadcasted_iota(I32, (tq, hd), 1)
    picks = [jnp.where((lane >= 3 * r) & (lane < 3 * r + 3), 1.0, 0.0).astype(BF16) for r in range(GROUP)]
    qa = jnp.concatenate([jnp.concatenate([qa_ref[:, r * hd:(r + 1) * hd], picks[r]], axis=1)
                          for r in range(GROUP)], axis=0)
    qb = jnp.concatenate([qb_ref[:, r * hd:(r + 1) * hd] for r in range(GROUP)], axis=0)

    def key_before_query(k0, n_keys, strict):
        key_pos = k0 + lax.broadcasted_iota(I32, (n_keys, w), 0)
        query_pos = q0 + lax.broadcasted_iota(I32, (n_keys, w), 1) % tq
        return key_pos < query_pos if strict else key_pos <= query_pos

    def fox_block(k0, n_keys, carry, diag):
        m, l, acc = carry
        s = _dot_nt(ka_ref[pl.ds(k0, n_keys), :], qa)
        if diag:
            s = jnp.where(key_before_query(k0, n_keys, False), s, NEG_INF)
        m_new = jnp.maximum(m, jnp.max(s, axis=0, keepdims=True))
        alpha = jnp.exp(m - m_new)
        p = jnp.exp(s - m_new)
        l = alpha * l + jnp.sum(p, axis=0, keepdims=True)
        acc = alpha * acc + _dot(vat_ref[:, pl.ds(k0, n_keys)], p.astype(BF16))
        return m_new, l, acc

    per_big = tk // tq
    carry = (jnp.full((1, w), NEG_INF, F32), jnp.zeros((1, w), F32), jnp.zeros((hd, w), F32))
    carry = lax.fori_loop(
        0, qi // per_big, lambda j, cr: fox_block(pl.multiple_of(j * tk, tk), tk, cr, False), carry)
    rest0 = (qi // per_big) * tk
    carry = lax.fori_loop(
        0, qi % per_big, lambda j, cr: fox_block(pl.multiple_of(rest0 + j * tq, tq), tq, cr, False), carry)
    _, l, acc = fox_block(pl.multiple_of(q0, tq), tq, carry, True)
    oa = acc / l

    r_ = lax.broadcasted_iota(I32, (tq, tq), 0)
    c_ = lax.broadcasted_iota(I32, (tq, tq), 1)
    later_keys = jnp.where(c_ > r_, 1.0, 0.0).astype(BF16)

    def sb_block(kj, c, acc, diag):
        k0 = pl.multiple_of(kj * tq, tq)
        z = _dot_nt(kb_ref[pl.ds(k0, tq), :], qb)
        t = jnp.log(1.0 + jnp.exp(-jnp.abs(z)))
        lneg = -jnp.maximum(z, 0.0) - t
        lpos = jnp.minimum(z, 0.0) - t
        if diag:
            visible = key_before_query(k0, tq, True)
            lneg = jnp.where(visible, lneg, 0.0)
        suf = c
        for piece in _split_bf16(lneg, 2):
            suf = suf + _dot(later_keys, piece)
        wgt = jnp.exp(lpos + suf)
        if diag:
            wgt = jnp.where(visible, wgt, 0.0)
        acc = acc + _dot(vbt_ref[:, pl.ds(k0, tq)], wgt.astype(BF16))
        c = c + jnp.sum(lneg, axis=0, keepdims=True)
        return c, acc

    c, acc = sb_block(qi, jnp.zeros((1, w), F32), jnp.zeros((hd, w), F32), True)

    def cond(st):
        return jnp.logical_and(st[0] >= 0, jnp.max(st[1]) > EXP_UNDERFLOW)

    def body(st):
        c2, acc2 = sb_block(st[0], st[1], st[2], False)
        return st[0] - 1, c2, acc2

    ob = lax.while_loop(cond, body, (qi - 1, c, acc))[2]

    for r in range(GROUP):
        ga = ga_ref[:, r * hd:(r + 1) * hd]
        gb = gb_ref[:, r * hd:(r + 1) * hd]
        oa_r = oa[:, r * tq:(r + 1) * tq].T
        ob_r = ob[:, r * tq:(r + 1) * tq].T
        o_ref[:, r * hd:(r + 1) * hd] = (_sigmoid(ga) * oa_r + _sigmoid(gb) * ob_r).astype(o_ref.dtype)


def _prompt_attention(qa, qb, ka_aug, va_t, kb, vb_t, gates, batch, seq, n_kv):
    n, dq = qa.shape
    tq = min(ATTN_TILE, seq)
    tk = ATTN_KEY_BLOCKS * tq
    nq = seq // tq
    gw = GROUP * HEAD_DIM
    qspec = pl.BlockSpec((tq, gw), lambda b, g, i: (b * nq + i, g))
    vspec = pl.BlockSpec((None, HEAD_DIM, seq), lambda b, g, i: (b, g, 0))
    return pl.pallas_call(
        functools.partial(_prompt_attn_kernel, tk=tk),
        out_shape=jax.ShapeDtypeStruct((n, dq), BF16),
        grid=(batch, n_kv, nq),
        in_specs=[qspec, qspec,
                  pl.BlockSpec((seq, 2 * HEAD_DIM), lambda b, g, i: (b, g)), vspec,
                  pl.BlockSpec((seq, HEAD_DIM), lambda b, g, i: (b, g)), vspec,
                  pl.BlockSpec((tq, gw), lambda b, g, i: (b * nq + i, g)),
                  pl.BlockSpec((tq, gw), lambda b, g, i: (b * nq + i, n_kv + g))],
        out_specs=qspec,
        compiler_params=_cparams("arbitrary", "arbitrary", "arbitrary"),
        name="prompt_attn",
    )(qa, qb, ka_aug, va_t, kb, vb_t, gates, gates)


def _block_diag_queries(q, n_heads, n_kv):
    rows = q.shape[0]
    kv_of_row = (lax.broadcasted_iota(I32, (rows, 1), 0) % n_heads) // GROUP
    blocks = [jnp.where(kv_of_row == g, q, jnp.zeros_like(q)) for g in range(n_kv)]
    return jnp.concatenate(blocks, axis=1), kv_of_row


def _own_head_block(pv, kv_of_row, n_kv):
    out = jnp.zeros((pv.shape[0], HEAD_DIM), F32)
    for g in range(n_kv):
        out = out + jnp.where(kv_of_row == g, pv[:, g * HEAD_DIM:(g + 1) * HEAD_DIM], 0.0)
    return out


def _page_copies(pt_ref, b, p, slot, kc_ref, vc_ref, kbuf, vbuf, sem):
    page = pt_ref[b, p]
    return (pltpu.make_async_copy(kc_ref.at[page], kbuf.at[slot], sem.at[0, slot]),
            pltpu.make_async_copy(vc_ref.at[page], vbuf.at[slot], sem.at[1, slot]))


def _load_page(buf, slot, n_kv):
    tokens = buf.shape[1] // n_kv
    heads = [buf[slot, pl.ds(g, tokens, stride=n_kv), :].astype(BF16) for g in range(n_kv)]
    return jnp.concatenate(heads, axis=1)


def _sample_fox_kernel(pt_ref, q_ref, kn_ref, vn_ref, bn_ref, *rest, n_pages, n_heads, n_kv, n_new):
    lf_refs = rest[:n_pages]
    kc_ref, vc_ref, o_ref, kbuf, vbuf, bias_s, sem = rest[n_pages:]
    b = pl.program_id(0)
    cols = kbuf.shape[1]
    page = cols // n_kv
    rows = q_ref.shape[0]
    last = n_pages - 1
    ring = kbuf.shape[0]
    total = pl.num_programs(0) * n_pages

    def copies(n):
        return _page_copies(pt_ref, n // n_pages, last - n % n_pages, n % ring, kc_ref, vc_ref, kbuf, vbuf, sem)

    @pl.when(b == 0)
    def _():
        for n in range(ring - 1):
            for cp in copies(n):
                cp.start()

    x = jnp.concatenate([lf_refs[p][...] for p in range(n_pages)], axis=0)
    key_of_col = lax.broadcasted_iota(I32, (page, cols), 1) // n_kv
    spread = jnp.where(lax.broadcasted_iota(I32, (page, cols), 0) > key_of_col, 1.0, 0.0).astype(BF16)
    within = jnp.zeros((x.shape[0], cols), F32)
    for piece in _split_bf16(x, 3):
        within = within + _dot(piece, spread)
    totals = jnp.sum(x, axis=1, keepdims=True)
    own = (lax.broadcasted_iota(I32, (n_heads, cols), 1) % n_kv) == (lax.broadcasted_iota(I32, (n_heads, cols), 0) // GROUP)
    other_head = jnp.where(own, 0.0, NEG_INF)
    later = jnp.zeros((n_heads, 1), F32)
    for p in range(last, -1, -1):
        bias_s[p] = within[p * n_heads:(p + 1) * n_heads] + later + other_head
        later = later + totals[p * n_heads:(p + 1) * n_heads]

    q = q_ref[...]

    def attend(kp, vp, bias, mask, carry):
        m, l, acc = carry
        s = _dot_nt(q, kp) + bias
        if mask is not None:
            s = jnp.where(mask, s, NEG_INF)
        m_new = jnp.maximum(m, jnp.max(s, axis=1, keepdims=True))
        alpha = jnp.exp(m - m_new)
        p = jnp.exp(s - m_new)
        l = alpha * l + jnp.sum(p, axis=1, keepdims=True)
        acc = alpha * acc + _dot(p.astype(BF16), vp)
        return m_new, l, acc

    init = (jnp.full((rows, 1), NEG_INF, F32), jnp.zeros((rows, 1), F32), jnp.zeros((rows, HEAD_DIM), F32))
    new_cols = kn_ref.shape[0]
    t_of_row = lax.broadcasted_iota(I32, (rows, new_cols), 0) // n_heads
    j_of_col = lax.broadcasted_iota(I32, (rows, new_cols), 1) // n_kv
    carry = attend(kn_ref[...].astype(BF16), vn_ref[...].astype(BF16),
                   jnp.concatenate([bn_ref[...]] * n_new, axis=0), j_of_col <= t_of_row, init)

    def body(i, carry):
        n = b * n_pages + i
        slot = n % ring
        for cp in copies(n):
            cp.wait()

        @pl.when(n + ring - 1 < total)
        def _():
            for cp in copies(n + ring - 1):
                cp.start()

        bias = jnp.concatenate([bias_s[last - i]] * n_new, axis=0)
        return attend(kbuf[slot].astype(BF16), vbuf[slot].astype(BF16), bias, None, carry)

    _, l, acc = lax.fori_loop(0, n_pages, body, carry)
    o_ref[...] = acc / l


def _sample_sb_kernel(pt_ref, q_ref, kn_ref, vn_ref, oa_ref, gate_ref, kc_ref, vc_ref, o_ref,
                      kbuf, vbuf, sem, *, n_pages, n_heads, n_kv, n_new):
    b = pl.program_id(0)
    page = kbuf.shape[1] // n_kv
    rows = q_ref.shape[0]
    last = n_pages - 1
    ahead = (kbuf.shape[0] - 2) // 2

    def ahead_copies(seq, j):
        return _page_copies(pt_ref, seq, last - j, (seq % 2) * ahead + j, kc_ref, vc_ref, kbuf, vbuf, sem)

    def copies(p):
        return _page_copies(pt_ref, b, p, 2 * ahead + p % 2, kc_ref, vc_ref, kbuf, vbuf, sem)

    @pl.when(b == 0)
    def _():
        for j in range(ahead):
            for cp in ahead_copies(0, j):
                cp.start()

    @pl.when(b + 1 < pl.num_programs(0))
    def _():
        for j in range(ahead):
            for cp in ahead_copies(b + 1, j):
                cp.start()

    qbd, kv_of_row = _block_diag_queries(q_ref[...], n_heads, n_kv)
    t_of_row = lax.broadcasted_iota(I32, (rows, page), 0) // n_heads
    lane = lax.broadcasted_iota(I32, (rows, page), 1)
    ones = _strict_lower_ones(page)

    def attend(kp, vp, mask, c, acc):
        z = _dot_nt(qbd, kp)
        t = jnp.log(1.0 + jnp.exp(-jnp.abs(z)))
        lneg = -jnp.maximum(z, 0.0) - t
        lpos = jnp.minimum(z, 0.0) - t
        if mask is not None:
            lneg = jnp.where(mask, lneg, 0.0)
        suf = c
        for piece in _split_bf16(lneg, 2):
            suf = suf + _dot(piece, ones)
        w = jnp.exp(lpos + suf)
        if mask is not None:
            w = jnp.where(mask, w, 0.0)
        acc = acc + _own_head_block(_dot(w.astype(BF16), vp), kv_of_row, n_kv)
        c = c + jnp.sum(lneg, axis=1, keepdims=True)
        return c, acc

    def pad_page(ref):
        new = ref[...]
        return jnp.concatenate([new, jnp.zeros((page - new.shape[0], new.shape[1]), F32)], axis=0).astype(BF16)

    c, acc = attend(pad_page(kn_ref), pad_page(vn_ref), lane < t_of_row,
                    jnp.zeros((rows, 1), F32), jnp.zeros((rows, HEAD_DIM), F32))

    def alive(c):
        return jnp.max(c) > EXP_UNDERFLOW

    for j in range(ahead):
        for cp in ahead_copies(b, j):
            cp.wait()
        slot = (b % 2) * ahead + j
        c, acc = lax.fori_loop(
            0, alive(c).astype(I32),
            lambda _, st: attend(_load_page(kbuf, slot, n_kv), _load_page(vbuf, slot, n_kv), None, st[0], st[1]),
            (c, acc))

    more = jnp.logical_and(alive(c), last - ahead >= 0)

    @pl.when(more)
    def _():
        for cp in copies(last - ahead):
            cp.start()

    def cond(st):
        return jnp.logical_and(st[0] >= 0, alive(st[1]))

    def body(st):
        p = st[0]
        for cp in copies(p):
            cp.wait()

        @pl.when(p > 0)
        def _():
            for cp in copies(p - 1):
                cp.start()

        slot = 2 * ahead + p % 2
        c2, acc2 = attend(_load_page(kbuf, slot, n_kv), _load_page(vbuf, slot, n_kv), None, st[1], st[2])
        return p - 1, c2, acc2

    p_end, _, ob = lax.while_loop(cond, body, (jnp.int32(last - ahead), c, acc))

    @pl.when(jnp.logical_and(more, p_end >= 0))
    def _():
        for cp in copies(jnp.maximum(p_end, 0)):
            cp.wait()

    ga = jnp.concatenate([gate_ref[t * 2 * n_heads:t * 2 * n_heads + n_heads, :] for t in range(n_new)], axis=0)
    gb = jnp.concatenate([gate_ref[t * 2 * n_heads + n_heads:(t + 1) * 2 * n_heads, :] for t in range(n_new)], axis=0)
    o_ref[...] = (_sigmoid(ga) * oa_ref[...] + _sigmoid(gb) * ob).astype(o_ref.dtype)


def _sample_attention(qa, qb, ka_flat, va_flat, kb_new, vb_new, bias_new, gates, lf_pool_t,
                      ck_fox, cv_fox, ck_sb, cv_sb, page_table, n_heads, n_kv, n_new):
    bs, n_pages = page_table.shape
    pool, page_rows, _ = ck_fox.shape
    page = page_rows // n_kv
    w = n_kv * HEAD_DIM
    rows = n_new * n_heads
    seq_rows = pl.BlockSpec((None, rows, HEAD_DIM), lambda b, pt: (b, 0, 0))
    new_flat = pl.BlockSpec((None, ka_flat.shape[1], HEAD_DIM), lambda b, pt: (b, 0, 0))
    new_kv = pl.BlockSpec((None, kb_new.shape[1], w), lambda b, pt: (b, 0, 0))
    any_spec = pl.BlockSpec(memory_space=pl.ANY)

    def page_buffers(n):
        return [pltpu.VMEM((n, page_rows, HEAD_DIM), F32), pltpu.VMEM((n, page_rows, HEAD_DIM), F32)]

    lf_specs = [pl.BlockSpec((None, n_heads, page), functools.partial(lambda b, pt, p: (pt[b, p], 0, 0), p=p))
                for p in range(n_pages)]
    oa = pl.pallas_call(
        functools.partial(_sample_fox_kernel, n_pages=n_pages, n_heads=n_heads, n_kv=n_kv, n_new=n_new),
        out_shape=jax.ShapeDtypeStruct((bs, rows, HEAD_DIM), F32),
        grid_spec=pltpu.PrefetchScalarGridSpec(
            num_scalar_prefetch=1, grid=(bs,),
            in_specs=[seq_rows, new_flat, new_flat,
                      pl.BlockSpec((None, n_heads, bias_new.shape[2]), lambda b, pt: (b, 0, 0))]
                     + lf_specs + [any_spec, any_spec],
            out_specs=seq_rows,
            scratch_shapes=page_buffers(FOX_RING) + [pltpu.VMEM((n_pages, n_heads, page_rows), F32),
                                                     pltpu.SemaphoreType.DMA((2, FOX_RING))]),
        compiler_params=_cparams("arbitrary"),
        name="sample_fox",
    )(page_table, qa, ka_flat, va_flat, bias_new, *([lf_pool_t] * n_pages), ck_fox, cv_fox)

    return pl.pallas_call(
        functools.partial(_sample_sb_kernel, n_pages=n_pages, n_heads=n_heads, n_kv=n_kv, n_new=n_new),
        out_shape=jax.ShapeDtypeStruct((bs, rows, HEAD_DIM), BF16),
        grid_spec=pltpu.PrefetchScalarGridSpec(
            num_scalar_prefetch=1, grid=(bs,),
            in_specs=[seq_rows, new_kv, new_kv, seq_rows,
                      pl.BlockSpec((None, 2 * rows, HEAD_DIM), lambda b, pt: (b, 0, 0)), any_spec, any_spec],
            out_specs=seq_rows,
            scratch_shapes=page_buffers(2 * SB_AHEAD + 2) + [pltpu.SemaphoreType.DMA((2, 2 * SB_AHEAD + 2))]),
        compiler_params=_cparams("arbitrary"),
        name="sample_sb",
    )(page_table, qb, kb_new, vb_new, oa, gates, ck_sb, cv_sb)


def _post_kernel(o_ref, wo_ref, x_ref, g1_ref, sh2_ref, sc2_ref, npm_ref, npf_ref, wr_ref, br_ref,
                 x1_ref, h2_ref, ti_ref, tg_ref):
    m = _dot(o_ref[...], wo_ref[...])
    x1 = x_ref[...] + g1_ref[...] * (_rms(m) * npm_ref[...])
    x1_ref[...] = x1
    h2 = _rms(x1) * npf_ref[...] * (1.0 + sc2_ref[...]) + sh2_ref[...]
    h2_ref[...] = h2
    logits = _dot(h2.astype(BF16), wr_ref[...]) + br_ref[...]
    lane = lax.broadcasted_iota(I32, logits.shape, 1)
    lane_f = lane.astype(F32)
    work = logits
    vals, ids = [], []
    for _ in range(TOP_K):
        mk = jnp.max(work, axis=1, keepdims=True)
        ik = jnp.min(jnp.where(work == mk, lane_f, float(LANES)), axis=1, keepdims=True)
        vals.append(mk)
        ids.append(ik)
        work = jnp.where(lane_f == ik, -jnp.inf, work)
    es = [jnp.exp(v - vals[0]) for v in vals]
    den = es[0]
    for e in es[1:]:
        den = den + e
    ti = jnp.zeros(logits.shape, F32)
    tg = jnp.zeros(logits.shape, F32)
    for k in range(TOP_K):
        ti = jnp.where(lane == k, ids[k], ti)
        tg = jnp.where(lane == k, es[k] / den, tg)
    ti_ref[...] = ti.astype(I32)
    tg_ref[...] = tg


def _post_attention(o, wo_bf, x, mod, npm, npf, wr_pad, br_pad):
    n, d = x.shape
    tm = min(POST_TILE, n)
    row = pl.BlockSpec((tm, d), lambda i: (i, 0))
    vec = pl.BlockSpec((1, d), lambda i: (0, 0))
    small = pl.BlockSpec((tm, LANES), lambda i: (i, 0))
    return pl.pallas_call(
        _post_kernel,
        out_shape=(jax.ShapeDtypeStruct((n, d), F32), jax.ShapeDtypeStruct((n, d), F32),
                   jax.ShapeDtypeStruct((n, LANES), I32), jax.ShapeDtypeStruct((n, LANES), F32)),
        grid=(n // tm,),
        in_specs=[row, pl.BlockSpec((d, d), lambda i: (0, 0)), row,
                  mod.spec(2, tm), mod.spec(3, tm), mod.spec(4, tm), vec, vec,
                  pl.BlockSpec((d, LANES), lambda i: (0, 0)), pl.BlockSpec((1, LANES), lambda i: (0, 0))],
        out_specs=(row, row, small, small),
        compiler_params=_cparams("arbitrary"),
        name="post_attn",
    )(o, wo_bf, x, mod.array, mod.array, mod.array, npm, npf, wr_pad, br_pad)


def _route_kernel(ti_ref, dest_ref, cnt_ref, seen, base, *, tm):
    ph = pl.program_id(0)
    i = pl.program_id(1)
    tr = ti_ref.shape[0]
    lane = lax.broadcasted_iota(I32, (tr, LANES), 1)
    ti = ti_ref[...]
    hits = [lane == ti[:, k:k + 1] for k in range(TOP_K)]
    member = jnp.zeros((tr, LANES), F32)
    for hit in hits:
        member = member + jnp.where(hit, 1.0, 0.0)
    tile_counts = jnp.sum(member, axis=0, keepdims=True)

    @pl.when(jnp.logical_and(ph == 0, i == 0))
    def _():
        seen[...] = jnp.zeros(seen.shape, F32)

    @pl.when(ph == 0)
    def _():
        seen[...] = seen[...] + tile_counts
        dest_ref[...] = jnp.zeros(dest_ref.shape, I32)

    @pl.when(jnp.logical_and(ph == 1, i == 0))
    def _():
        counts = seen[...]
        cnt_ref[...] = jnp.broadcast_to(counts, cnt_ref.shape)
        tiles = jnp.floor((counts + (tm - 1)) * (1.0 / tm))
        r = lax.broadcasted_iota(I32, (LANES, LANES), 0)
        c = lax.broadcasted_iota(I32, (LANES, LANES), 1)
        before = jnp.where(r < c, 1.0, 0.0).astype(BF16)
        first_tile = jnp.zeros((1, LANES), F32)
        for piece in _split_bf16(jnp.broadcast_to(tiles, (8, LANES)), 2):
            first_tile = first_tile + _dot(piece, before)[0:1]
        base[...] = first_tile * tm
        seen[...] = jnp.zeros(seen.shape, F32)

    @pl.when(ph == 1)
    def _():
        r = lax.broadcasted_iota(I32, (tr, tr), 0)
        c = lax.broadcasted_iota(I32, (tr, tr), 1)
        earlier = jnp.where(c < r, 1.0, 0.0).astype(BF16)
        rank = _dot(earlier, member.astype(BF16)) + (base[...] + seen[...])
        dest = jnp.zeros((tr, LANES), F32)
        for k, hit in enumerate(hits):
            dest = jnp.where(lane == k, jnp.sum(jnp.where(hit, rank, 0.0), axis=1, keepdims=True), dest)
        dest_ref[...] = dest.astype(I32)
        seen[...] = seen[...] + tile_counts


def _route(top_i, tm):
    n = top_i.shape[0]
    tr = next(t for t in (256, 128, 64, 32, 16, 8) if n % t == 0)
    return pl.pallas_call(
        functools.partial(_route_kernel, tm=tm),
        out_shape=(jax.ShapeDtypeStruct((n, LANES), I32), jax.ShapeDtypeStruct((8, LANES), F32)),
        grid=(2, n // tr),
        in_specs=[pl.BlockSpec((tr, LANES), lambda ph, i: (i, 0))],
        out_specs=(pl.BlockSpec((tr, LANES), lambda ph, i: (i * ph, 0)), pl.BlockSpec((8, LANES), lambda ph, i: (0, 0))),
        scratch_shapes=[pltpu.VMEM((1, LANES), F32), pltpu.VMEM((1, LANES), F32)],
        compiler_params=_cparams("arbitrary", "arbitrary"),
        name="moe_route",
    )(top_i)


def _row_copy(src, src_row, dst, dst_row, sem):
    return pltpu.make_async_copy(src.at[pl.ds(src_row, 1), :], dst.at[pl.ds(dst_row, 1), :], sem)


def _dispatch_kernel(pad_ref, dest_ref, hp_ref, hs_ref, xs_hbm, zero, sem, *, steps_p):
    i = pl.program_id(0)
    tm = hp_ref.shape[0]

    def for_pad_rows(fn):
        def group(e, carry):
            def row(j, carry2):
                fn(pad_ref[0, e] + j)
                return carry2
            return lax.fori_loop(0, pad_ref[1, e], row, carry)
        lax.fori_loop(0, pad_ref.shape[1], group, 0)

    @pl.when(i == 0)
    def _():
        zero[...] = jnp.zeros(zero.shape, F32)
        for_pad_rows(lambda row: _row_copy(zero, 0, xs_hbm, row, sem.at[1]).start())
        for_pad_rows(lambda row: _row_copy(zero, 0, xs_hbm, 0, sem.at[1]).wait())

    def scatter(h_ref):
        def body(r, carry):
            for k in range(TOP_K):
                _row_copy(h_ref, r, xs_hbm, dest_ref[0, r * TOP_K + k], sem.at[0]).start()
            return carry
        lax.fori_loop(0, tm, body, 0, unroll=4)
        for k in range(TOP_K):
            pltpu.make_async_copy(h_ref, xs_hbm.at[pl.ds(0, tm), :], sem.at[0]).wait()

    @pl.when(i < steps_p)
    def _():
        scatter(hp_ref)

    @pl.when(i >= steps_p)
    def _():
        scatter(hs_ref)


def _dispatch(h2_p, h2_s, dest, pad_rows, n_rows):
    (np_, d), ns = h2_p.shape, h2_s.shape[0]
    tm = next(t for t in (256, 128, 64, 32, 16, 8) if np_ % t == 0 and ns % t == 0)
    steps_p, steps = np_ // tm, (np_ + ns) // tm
    dest3 = dest[:, :TOP_K].reshape(steps, 1, tm * TOP_K)
    return pl.pallas_call(
        functools.partial(_dispatch_kernel, steps_p=steps_p),
        out_shape=jax.ShapeDtypeStruct((n_rows, d), F32),
        grid_spec=pltpu.PrefetchScalarGridSpec(
            num_scalar_prefetch=1, grid=(steps,),
            in_specs=[pl.BlockSpec((None, 1, tm * TOP_K), lambda i, pad: (i, 0, 0), memory_space=pltpu.SMEM),
                      pl.BlockSpec((tm, d), lambda i, pad: (jnp.minimum(i, steps_p - 1), 0)),
                      pl.BlockSpec((tm, d), lambda i, pad: (jnp.maximum(i - steps_p, 0), 0))],
            out_specs=pl.BlockSpec(memory_space=pl.ANY),
            scratch_shapes=[pltpu.VMEM((8, d), F32), pltpu.SemaphoreType.DMA((2,))]),
        compiler_params=_cparams("arbitrary"),
        name="moe_dispatch",
    )(pad_rows, dest3, h2_p, h2_s)


def _new_expert(te_ref, t):
    prev = te_ref[jnp.maximum(t - 1, 0)]
    return jnp.logical_or(t == 0, te_ref[t] != prev)


def _moe_up_kernel(te_ref, nu_ref, xs_ref, w_ref, b_ref, act_ref, wbf):
    t = pl.program_id(1)

    @pl.when(t < nu_ref[0])
    def _():
        @pl.when(_new_expert(te_ref, t))
        def _():
            wbf[...] = w_ref[...].astype(BF16)

        gu = _dot(xs_ref[...].astype(BF16), wbf[...]) + b_ref[...]
        tn = gu.shape[1]
        nxt = pltpu.roll(gu, tn - 1, 1)
        glu = jnp.minimum(gu, SWIGLU_LIMIT)
        lin = jnp.clip(nxt, -SWIGLU_LIMIT, SWIGLU_LIMIT)
        act = glu * _sigmoid(SWIGLU_ALPHA * glu) * (lin + 1.0)
        even = lax.broadcasted_iota(I32, act.shape, 1) % 2 == 0
        act = jnp.where(even, act, 0.0).astype(BF16)
        blk = 2 * LANES
        r = lax.broadcasted_iota(I32, (blk, LANES), 0)
        c = lax.broadcasted_iota(I32, (blk, LANES), 1)
        pick = jnp.where(r == 2 * c, 1.0, 0.0).astype(BF16)
        for j in range(tn // blk):
            act_ref[:, j * LANES:(j + 1) * LANES] = _dot(act[:, j * blk:(j + 1) * blk], pick).astype(act_ref.dtype)

    @pl.when(t >= nu_ref[0])
    def _():
        act_ref[...] = jnp.zeros(act_ref.shape, act_ref.dtype)


def _moe_up(xs, w_up, b_up, tile_expert, n_used, tm, n_tiles):
    n_experts, d, two_f = w_up.shape
    tn = min(MOE_UP_COLS, two_f)

    def tile(t, nu):
        return jnp.minimum(t, nu[0] - 1)
    return pl.pallas_call(
        _moe_up_kernel,
        out_shape=jax.ShapeDtypeStruct((n_tiles * tm, two_f // 2), BF16),
        grid_spec=pltpu.PrefetchScalarGridSpec(
            num_scalar_prefetch=2, grid=(two_f // tn, n_tiles),
            in_specs=[pl.BlockSpec((tm, d), lambda f, t, te, nu: (tile(t, nu), 0)),
                      pl.BlockSpec((None, d, tn), lambda f, t, te, nu: (te[tile(t, nu)], 0, f)),
                      pl.BlockSpec((None, 1, tn), lambda f, t, te, nu: (te[tile(t, nu)], 0, f))],
            out_specs=pl.BlockSpec((tm, tn // 2), lambda f, t, te, nu: (t, f)),
            scratch_shapes=[pltpu.VMEM((d, tn), BF16)]),
        compiler_params=_cparams("arbitrary", "arbitrary"),
        name="moe_up",
    )(tile_expert, n_used, xs, w_up, b_up.reshape(n_experts, 1, two_f))


def _moe_down_kernel(te_ref, nu_ref, act_ref, w_ref, b_ref, y_ref, wbf):
    t = pl.program_id(1)

    @pl.when(t < nu_ref[0])
    def _():
        @pl.when(_new_expert(te_ref, t))
        def _():
            wbf[...] = w_ref[...].astype(BF16)

        y_ref[...] = _dot(act_ref[...], wbf[...]) + b_ref[...]

    @pl.when(t >= nu_ref[0])
    def _():
        y_ref[...] = jnp.zeros(y_ref.shape, y_ref.dtype)


def _moe_down(act, w_down, b_down, tile_expert, n_used, tm, n_tiles):
    n_experts, f, d = w_down.shape
    tn = min(MOE_DOWN_COLS, d)

    def tile(t, nu):
        return jnp.minimum(t, nu[0] - 1)
    return pl.pallas_call(
        _moe_down_kernel,
        out_shape=jax.ShapeDtypeStruct((n_tiles * tm, d), F32),
        grid_spec=pltpu.PrefetchScalarGridSpec(
            num_scalar_prefetch=2, grid=(d // tn, n_tiles),
            in_specs=[pl.BlockSpec((tm, f), lambda j, t, te, nu: (tile(t, nu), 0)),
                      pl.BlockSpec((None, f, tn), lambda j, t, te, nu: (te[tile(t, nu)], 0, j)),
                      pl.BlockSpec((None, 1, tn), lambda j, t, te, nu: (te[tile(t, nu)], 0, j))],
            out_specs=pl.BlockSpec((tm, tn), lambda j, t, te, nu: (t, j)),
            scratch_shapes=[pltpu.VMEM((f, tn), BF16)]),
        compiler_params=_cparams("arbitrary", "arbitrary"),
        name="moe_down",
    )(tile_expert, n_used, act, w_down, b_down.reshape(n_experts, 1, d))


def _combine_kernel(dest_ref, dest_next_ref, ys_hbm, x1_ref, gate_ref, g2_ref, npost_ref, y_ref, buf, sem):
    i = pl.program_id(0)
    n_steps = pl.num_programs(0)
    tm = x1_ref.shape[0]

    def issue(idx_ref, slot):
        def body(r, carry):
            for k in range(TOP_K):
                _row_copy(ys_hbm, idx_ref[0, r * TOP_K + k], buf.at[slot, k], r, sem.at[slot]).start()
            return carry
        lax.fori_loop(0, tm, body, 0, unroll=4)

    @pl.when(i == 0)
    def _():
        issue(dest_ref, 0)

    @pl.when(i + 1 < n_steps)
    def _():
        issue(dest_next_ref, (i + 1) % 2)

    slot = i % 2
    for k in range(TOP_K):
        pltpu.make_async_copy(ys_hbm.at[pl.ds(0, tm), :], buf.at[slot, k], sem.at[slot]).wait()
    gate = gate_ref[...]
    moe = gate[:, 0:1] * buf[slot, 0]
    for k in range(1, TOP_K):
        moe = moe + gate[:, k:k + 1] * buf[slot, k]
    y_ref[...] = x1_ref[...] + g2_ref[...] * (_rms(moe) * npost_ref[...])


def _moe_combine(ys, dest, gates, x1, mod, npost):
    n, d = x1.shape
    tm = min(COMBINE_TILE, n)
    steps = n // tm
    dest3 = dest[:, :TOP_K].reshape(steps, 1, tm * TOP_K)
    row = pl.BlockSpec((tm, d), lambda i: (i, 0))
    return pl.pallas_call(
        _combine_kernel,
        out_shape=jax.ShapeDtypeStruct((n, d), F32),
        grid=(steps,),
        in_specs=[pl.BlockSpec((None, 1, tm * TOP_K), lambda i: (i, 0, 0), memory_space=pltpu.SMEM),
                  pl.BlockSpec((None, 1, tm * TOP_K), lambda i: (jnp.minimum(i + 1, steps - 1), 0, 0),
                               memory_space=pltpu.SMEM),
                  pl.BlockSpec(memory_space=pl.ANY), row, pl.BlockSpec((tm, LANES), lambda i: (i, 0)),
                  mod.spec(5, tm), pl.BlockSpec((1, d), lambda i: (0, 0))],
        out_specs=row,
        scratch_shapes=[pltpu.VMEM((2, TOP_K, tm, d), F32), pltpu.SemaphoreType.DMA((2,))],
        compiler_params=_cparams("arbitrary"),
        name="moe_combine",
    )(dest3, dest3, ys, x1, gates, mod.array, npost)


def _layer(l, x_prompt, x_sample, c_prompt, c_sample, cache_k_fox, cache_v_fox, cache_logf_fox,
           cache_k_sb, cache_v_sb, page_table, w_ada, b_ada, norm_pre_mix, norm_post_mix,
           norm_pre_ffn, norm_post_ffn, w_in, b_forget, w_out, w_router, b_router,
           w_up, b_up, w_down, b_down):
    batch, seq, d = x_prompt.shape
    bs, n_new, _ = x_sample.shape
    n_heads = d // HEAD_DIM
    n_kv = n_heads // GROUP
    kv_w = n_kv * HEAD_DIM
    n_experts = w_router.shape[-1]
    pool, page = cache_k_fox.shape[1], cache_k_fox.shape[2]
    np_, ns = batch * seq, bs * n_new

    f0 = d + 2 * kv_w
    w_l = w_in[l]
    w_main = jnp.concatenate([w_l[:, :f0], w_l[:, f0 + n_heads:]], axis=1).astype(BF16)
    wf_pad = jnp.pad(w_l[:, f0:f0 + n_heads], ((0, 0), (0, LANES - n_heads))).astype(BF16)
    bf_pad = jnp.pad(b_forget[l], (0, LANES - n_heads)).reshape(1, LANES)
    wo_bf = w_out[l].astype(BF16)
    wr_pad = jnp.pad(w_router[l], ((0, 0), (0, LANES - n_experts))).astype(BF16)
    br_pad = jnp.pad(b_router[l], (0, LANES - n_experts), constant_values=NEG_INF).reshape(1, LANES)
    norms = [n_[l].reshape(1, d) for n_ in (norm_pre_mix, norm_post_mix, norm_pre_ffn, norm_post_ffn)]

    n_c = batch + bs
    c_all = jnp.concatenate([c_prompt, c_sample, jnp.zeros((-n_c % 8, d), F32)], axis=0)
    mod_all = _modulation(c_all, w_ada[l], b_ada[l])
    mod_p = _prompt_mod(mod_all[:batch], seq, d)
    mod_s = _sample_mod(mod_all[batch:n_c], n_new, d)

    xp = x_prompt.reshape(np_, d)
    xs = x_sample.reshape(ns, d)
    scale = HEAD_DIM ** -0.5
    col = {"qa": 0, "ka": d, "va": d + kv_w, "qb": f0, "kb": f0 + d, "vb": f0 + d + kv_w, "gate": f0 + d + 2 * kv_w}

    def mixer_inputs(x, mod):
        h, logf = _prenorm(x, mod, norms[0], wf_pad, bf_pad, n_heads)
        qa, = _project(h, w_main, col["qa"], d, (BF16,), scale)
        qb, = _project(h, w_main, col["qb"], d, (BF16,), scale)
        ka = _project(h, w_main, col["ka"], kv_w, (F32, BF16))
        va = _project(h, w_main, col["va"], kv_w, (F32, BF16))
        kb = _project(h, w_main, col["kb"], kv_w, (F32, BF16))
        vb = _project(h, w_main, col["vb"], kv_w, (F32, BF16))
        gates, = _project(h, w_main, col["gate"], 2 * d, (F32,))
        return logf, qa, qb, ka, va, kb, vb, gates

    logf_p, qa, qb, ka, va, kb, vb, gates = mixer_inputs(xp, mod_p)
    f_rows = _cumsum_rows(logf_p.reshape(batch, seq, n_heads).transpose(0, 2, 1))
    neg_f = -f_rows.transpose(0, 2, 1).reshape(batch, seq, n_kv, GROUP)
    pieces = []
    for _ in range(2):
        top = lax.bitcast_convert_type(
            lax.bitcast_convert_type(neg_f, jnp.uint32) & jnp.uint32(0xFFFF0000), F32)
        pieces.append(top.astype(BF16))
        neg_f = neg_f - top
    pieces.append(neg_f.astype(BF16))
    f_cols = jnp.stack(pieces, axis=-1).reshape(batch, seq, n_kv, 3 * GROUP)
    ka_aug = jnp.concatenate(
        [ka[1].reshape(batch, seq, n_kv, HEAD_DIM), f_cols,
         jnp.zeros((batch, seq, n_kv, HEAD_DIM - 3 * GROUP), BF16)], axis=-1).reshape(np_, 2 * kv_w)
    to_rows = lambda v: v.reshape(batch, seq, kv_w).transpose(0, 2, 1)
    o_p = _prompt_attention(qa, qb, ka_aug, to_rows(va[1]), kb[1], to_rows(vb[1]), gates, batch, seq, n_kv)
    state_p = (ka[0].reshape(batch, seq, n_kv, HEAD_DIM), va[0].reshape(batch, seq, n_kv, HEAD_DIM),
               logf_p.reshape(batch, seq, n_heads),
               kb[0].reshape(batch, seq, n_kv, HEAD_DIM), vb[0].reshape(batch, seq, n_kv, HEAD_DIM))

    logf_s, qa, qb, ka, va, kb, vb, gates = mixer_inputs(xs, mod_s)
    rows = n_new * n_heads
    pad_new = lambda a: jnp.pad(a.reshape(bs, n_new, kv_w), ((0, 0), (0, -n_new % 8), (0, 0)))
    new_tok = LANES // n_kv
    flat_new = lambda a: jnp.pad(a.reshape(bs, n_new, n_kv, HEAD_DIM),
                                 ((0, 0), (0, new_tok - n_new), (0, 0), (0, 0))).reshape(bs, LANES, HEAD_DIM)
    lf_s = logf_s.reshape(bs, n_new, n_heads)
    own = (jnp.arange(LANES, dtype=I32)[None, :] % n_kv) == (jnp.arange(n_heads, dtype=I32)[:, None] // GROUP)
    bias_new = jnp.pad(jnp.repeat(-jnp.cumsum(lf_s, axis=1).transpose(0, 2, 1), n_kv, axis=2),
                       ((0, 0), (0, 0), (0, LANES - n_new * n_kv)))
    bias_new = jnp.where(own[None], bias_new, NEG_INF)
    o_s = _sample_attention(
        qa.reshape(bs, rows, HEAD_DIM), qb.reshape(bs, rows, HEAD_DIM),
        flat_new(ka[0]), flat_new(va[0]), pad_new(kb[0]), pad_new(vb[0]), bias_new,
        gates.reshape(bs, 2 * rows, HEAD_DIM), cache_logf_fox[l].transpose(0, 2, 1),
        *(c[l].reshape(pool, page * n_kv, HEAD_DIM) for c in (cache_k_fox, cache_v_fox, cache_k_sb, cache_v_sb)),
        page_table, n_heads, n_kv, n_new).reshape(ns, d)
    state_s = (ka[0].reshape(bs, n_new, n_kv, HEAD_DIM), va[0].reshape(bs, n_new, n_kv, HEAD_DIM), lf_s,
               kb[0].reshape(bs, n_new, n_kv, HEAD_DIM), vb[0].reshape(bs, n_new, n_kv, HEAD_DIM))

    x1_p, h2_p, ti_p, tg_p = _post_attention(o_p, wo_bf, xp, mod_p, norms[1], norms[2], wr_pad, br_pad)
    x1_s, h2_s, ti_s, tg_s = _post_attention(o_s, wo_bf, xs, mod_s, norms[1], norms[2], wr_pad, br_pad)

    n_tok = np_ + ns
    tm = MOE_TILE
    n_tiles = (n_tok * TOP_K) // tm + n_experts
    dest, counts = _route(jnp.concatenate([ti_p, ti_s], axis=0), tm)
    counts = counts[0, :n_experts].astype(I32)
    tiles_e = (counts + tm - 1) // tm
    tile_end = jnp.cumsum(tiles_e)
    n_used = tile_end[-1:]
    tile_expert = jnp.minimum(jnp.sum(tile_end[None, :] <= jnp.arange(n_tiles, dtype=I32)[:, None], axis=1),
                              n_experts - 1).astype(I32)
    group_start = (tile_end - tiles_e) * tm
    pad_rows = jnp.stack([jnp.append(group_start + counts, n_used[0] * tm),
                          jnp.append(tiles_e * tm - counts, (n_tiles - n_used[0]) * tm)]).astype(I32)
    xsort = _dispatch(h2_p, h2_s, dest, pad_rows, n_tiles * tm)
    act = _moe_up(xsort, w_up[l], b_up[l], tile_expert, n_used, tm, n_tiles)
    ys = _moe_down(act, w_down[l], b_down[l], tile_expert, n_used, tm, n_tiles)
    y_p = _moe_combine(ys, dest[:np_], tg_p, x1_p, mod_p, norms[3])
    y_s = _moe_combine(ys, dest[np_:], tg_s, x1_s, mod_s, norms[3])
    return y_p.reshape(batch, seq, d), y_s.reshape(bs, n_new, d), state_p, state_s


def kernel(x_prompt, x_sample, c_prompt, c_sample, cache_k_fox, cache_v_fox, cache_logf_fox, cache_k_sb, cache_v_sb, page_table, w_ada, b_ada, norm_pre_mix, norm_post_mix, norm_pre_ffn, norm_post_ffn, w_in, b_forget, w_out, w_router, b_router, w_up, b_up, w_down, b_down):
    depth = w_in.shape[0]
    xp, xs = x_prompt, x_sample
    p_states, s_states = [], []
    for l in range(depth):
        xp, xs, sp, ss = _layer(l, xp, xs, c_prompt, c_sample, cache_k_fox, cache_v_fox, cache_logf_fox,
                                cache_k_sb, cache_v_sb, page_table, w_ada, b_ada, norm_pre_mix, norm_post_mix,
                                norm_pre_ffn, norm_post_ffn, w_in, b_forget, w_out, w_router, b_router,
                                w_up, b_up, w_down, b_down)
        p_states.append(sp)
        s_states.append(ss)
    k_fox_p, v_fox_p, logf_fox_p, k_sb_p, v_sb_p = (jnp.stack(t) for t in zip(*p_states))
    k_fox_s, v_fox_s, logf_fox_s, k_sb_s, v_sb_s = (jnp.stack(t) for t in zip(*s_states))
    return (xp, xs, k_fox_p, v_fox_p, logf_fox_p, k_sb_p, v_sb_p,
            k_fox_s, v_fox_s, logf_fox_s, k_sb_s, v_sb_s)
```
